```python
import math
import jax
import jax.numpy as jnp
from jax import lax
import numpy as np

D_MODEL = 2048
BATCH = 1
SEQ = 8192
DEPTH = 2
DEC_BATCH = 128
DEC_SEQ = 4
PAST_LEN = 2048
PAGE_SIZE = 128

HEAD_DIM = 128
D_INNER = 3 * D_MODEL // 4
MEM_WIDTH = D_MODEL // 4
N_HEADS_A = D_INNER // HEAD_DIM
KV_HEADS = 4
Q_PER_KV = N_HEADS_A // KV_HEADS
IDX_HEADS = 16
IDX_DIM = 64
TOPK_MAX = 256
Q_BLOCK = 128
REL_BUCKETS = 32
REL_MAX_EXACT = 16
REL_MAX_DIST = 128
MEM_TOKENS = 256
MEM_HEADS = 4
MEM_HEAD_DIM = MEM_WIDTH // MEM_HEADS
SSM_HEADDIM = 64
SSM_HEADS = D_INNER // SSM_HEADDIM
SSM_GROUPS = 4
D_STATE = 128
CONV_W = 4
CONV_DIM = D_INNER + 2 * SSM_GROUPS * D_STATE
SSD_CHUNK = 128
D_FF = (8 * D_MODEL // 3 + 127) // 128 * 128
ATTN_IN = N_HEADS_A * HEAD_DIM + 2 * KV_HEADS * HEAD_DIM + IDX_HEADS * IDX_DIM + IDX_DIM + IDX_HEADS + MEM_WIDTH
SSM_IN = D_INNER + CONV_DIM + SSM_HEADS + MEM_WIDTH
N_ATTN_LAYERS = (DEPTH + 1) // 2
N_SSD_LAYERS = DEPTH // 2
EPS = 1e-6

kernel_name = 'hybrid_dsa_ssd_macaron_step'


def rms_norm(x, g):
    xf = x.astype(jnp.float32)
    y = xf * lax.rsqrt(jnp.mean(xf * xf, axis=-1, keepdims=True) + EPS)
    return (y * g.astype(jnp.float32)).astype(x.dtype)


def swiglu(x, w_gu, w_d):
    gate, up = jnp.split(x @ w_gu, 2, axis=-1)
    return (jax.nn.silu(gate) * up) @ w_d


def split_attn(z):
    b, t = z.shape[:2]
    qw, kvw = N_HEADS_A * HEAD_DIM, KV_HEADS * HEAD_DIM
    cuts = [int(c) for c in np.cumsum([qw, kvw, kvw, IDX_HEADS * IDX_DIM, IDX_DIM, IDX_HEADS])]
    q, k, v, qi, ki, wi, qm = jnp.split(z, cuts, axis=-1)
    return (q.reshape(b, t, KV_HEADS, Q_PER_KV, HEAD_DIM), k.reshape(b, t, KV_HEADS, HEAD_DIM),
            v.reshape(b, t, KV_HEADS, HEAD_DIM), qi.reshape(b, t, IDX_HEADS, IDX_DIM), ki, wi,
            qm.reshape(b, t, MEM_HEADS, MEM_HEAD_DIM))


def t5_bucket(dist):
    n = jnp.maximum(dist, 0)
    nf = jnp.maximum(n, 1).astype(jnp.float32)
    large = REL_MAX_EXACT + (jnp.log(nf / REL_MAX_EXACT) / math.log(REL_MAX_DIST / REL_MAX_EXACT)
                             * (REL_BUCKETS - REL_MAX_EXACT)).astype(jnp.int32)
    large = jnp.minimum(large, REL_BUCKETS - 1)
    return jnp.where(n < REL_MAX_EXACT, n, large)


def indexer_topk(q_idx, w_idx, k_idx, qpos, topk):
    s = jax.nn.relu(jnp.einsum('bthd,bsd->bths', q_idx, k_idx).astype(jnp.float32) * IDX_DIM ** -0.5)
    score = jnp.einsum('bths,bth->bts', s, w_idx.astype(jnp.float32)) * IDX_HEADS ** -0.5
    kpos = jnp.arange(k_idx.shape[1], dtype=jnp.int32)
    score = jnp.where(kpos[None, None, :] <= qpos[None, :, None], score, -jnp.inf)
    _, idx = lax.top_k(score, topk)
    valid = idx <= qpos[None, :, None]
    return idx, valid


def sparse_attend(q, k_sel, v_sel, idx, valid, qpos, rel_bias):
    b, t, kk = idx.shape
    logits = jnp.einsum('btgrd,btkgd->btgrk', q, k_sel).astype(jnp.float32) * HEAD_DIM ** -0.5
    bias = rel_bias[t5_bucket(qpos[None, :, None] - idx)].astype(jnp.float32)
    bias = bias.reshape(b, t, kk, KV_HEADS, Q_PER_KV).transpose(0, 1, 3, 4, 2)
    logits = jnp.where(valid[:, :, None, None, :], logits + bias, -jnp.inf)
    p = jax.nn.softmax(logits, axis=-1).astype(v_sel.dtype)
    return jnp.einsum('btgrk,btkgd->btgrd', p, v_sel)


def dsa_mixer_prompt(h, w_in, rel_bias):
    b, s_len, _ = h.shape
    q, k, v, qi, ki, wi, qm = split_attn(h @ w_in)
    topk = min(TOPK_MAX, s_len // 4)
    take = jax.vmap(lambda a, ii: a[ii])

    def block(i):
        start = i * Q_BLOCK
        sl = lambda a: lax.dynamic_slice_in_dim(a, start, Q_BLOCK, axis=1)
        qpos = start + jnp.arange(Q_BLOCK, dtype=jnp.int32)
        idx, valid = indexer_topk(sl(qi), sl(wi), ki, qpos, topk)
        return sparse_attend(sl(q), take(k, idx), take(v, idx), idx, valid, qpos, rel_bias)

    out = lax.map(block, jnp.arange(s_len // Q_BLOCK, dtype=jnp.int32))
    out = jnp.moveaxis(out, 0, 1).reshape(b, s_len, N_HEADS_A * HEAD_DIM)
    return out, qm, k, v, ki


def dsa_mixer_sample(h, w_in, rel_bias, cache_k, cache_v, cache_kidx, page_table):
    db, t, _ = h.shape
    q, k, v, qi, ki, wi, qm = split_attn(h @ w_in)
    past = page_table.shape[1] * PAGE_SIZE
    topk = min(TOPK_MAX, (past + t) // 4)
    ki_past = cache_kidx[page_table].reshape(db, past, IDX_DIM)
    ki_all = jnp.concatenate([ki_past.astype(ki.dtype), ki], axis=1)
    qpos = past + jnp.arange(t, dtype=jnp.int32)
    idx, valid = indexer_topk(qi, wi, ki_all, qpos, topk)
    in_past = (idx < past)[..., None, None]
    pidx = jnp.minimum(idx, past - 1)
    phys = jax.vmap(lambda pt, pg: pt[pg])(page_table, pidx // PAGE_SIZE)
    off = pidx % PAGE_SIZE
    nidx = jnp.clip(idx - past, 0, t - 1)
    take = jax.vmap(lambda a, ii: a[ii])
    k_sel = jnp.where(in_past, cache_k[phys, off].astype(k.dtype), take(k, nidx))
    v_sel = jnp.where(in_past, cache_v[phys, off].astype(v.dtype), take(v, nidx))
    out = sparse_attend(q, k_sel, v_sel, idx, valid, qpos, rel_bias)
    return out.reshape(db, t, N_HEADS_A * HEAD_DIM), qm, k, v, ki


def causal_conv(xbc, conv_state, w, bias):
    t = xbc.shape[1]
    xpad = jnp.concatenate([conv_state.astype(xbc.dtype), xbc], axis=1)
    out = bias
    for j in range(CONV_W):
        out = out + xpad[:, j:j + t] * w[j]
    return jax.nn.silu(out), xpad[:, -(CONV_W - 1):]


def ssd_chunked(x, da, bm, cm, h0, chunk):
    b, t, nh, p = x.shape
    g, n = bm.shape[2], bm.shape[3]
    hg = nh // g
    c = t // chunk
    x = x.reshape(b, c, chunk, g, hg, p)
    bm = bm.reshape(b, c, chunk, g, n)
    cm = cm.reshape(b, c, chunk, g, n)
    acs = jnp.cumsum(da.reshape(b, c, chunk, g, hg).transpose(0, 1, 3, 4, 2), axis=-1)
    causal = jnp.tril(jnp.ones((chunk, chunk), dtype=bool))
    seg = acs[..., :, None] - acs[..., None, :]
    lmat = jnp.where(causal, jnp.exp(jnp.where(causal, seg, 0.0)), 0.0)
    cb = jnp.einsum('bclgn,bcsgn->bcgls', cm, bm)
    y_diag = jnp.einsum('bcghls,bcsghp->bclghp', cb[:, :, :, None] * lmat, x)
    decay_states = jnp.exp(acs[..., -1:] - acs).transpose(0, 1, 4, 2, 3)
    states = jnp.einsum('bclgn,bclghp->bcghpn', bm, x * decay_states[..., None])
    chunk_decay = jnp.exp(acs[..., -1])

    def step(hs, inp):
        st, dec = inp
        return hs * dec[..., None, None] + st, hs

    h_final, h_in = lax.scan(step, h0.reshape(b, g, hg, p, n),
                             (jnp.moveaxis(states, 1, 0), jnp.moveaxis(chunk_decay, 1, 0)))
    h_in = jnp.moveaxis(h_in, 0, 1)
    y_off = jnp.einsum('bclgn,bcghpn->bclghp', cm, h_in) * jnp.exp(acs).transpose(0, 1, 4, 2, 3)[..., None]
    return (y_diag + y_off).reshape(b, t, nh, p), h_final.reshape(b, nh, p, n)


def gated_rmsnorm(y, z, g):
    b, t, _ = y.shape
    yg = (y * jax.nn.silu(z.astype(jnp.float32))).reshape(b, t, SSM_GROUPS, -1)
    yg = yg * lax.rsqrt(jnp.mean(yg * yg, axis=-1, keepdims=True) + EPS)
    return yg.reshape(b, t, D_INNER) * g.astype(jnp.float32)


def ssd_mixer(h, w_in, conv_w, conv_b, dt_bias, a_log, d_skip, norm_g, conv_state, ssm_state):
    b, t, _ = h.shape
    z, xbc, dt_raw, qm = jnp.split(h @ w_in, [D_INNER, D_INNER + CONV_DIM, D_INNER + CONV_DIM + SSM_HEADS], axis=-1)
    xbc, new_conv = causal_conv(xbc, conv_state, conv_w, conv_b)
    xs, bm, cm = jnp.split(xbc, [D_INNER, D_INNER + SSM_GROUPS * D_STATE], axis=-1)
    xs = xs.reshape(b, t, SSM_HEADS, SSM_HEADDIM).astype(jnp.float32)
    bm = bm.reshape(b, t, SSM_GROUPS, D_STATE).astype(jnp.float32)
    cm = cm.reshape(b, t, SSM_GROUPS, D_STATE).astype(jnp.float32)
    dt = jax.nn.softplus(dt_raw.astype(jnp.float32) + dt_bias.astype(jnp.float32))
    a = -jnp.exp(a_log.astype(jnp.float32))
    chunk = min(SSD_CHUNK, t)
    y, h_final = ssd_chunked(xs * dt[..., None], dt * a, bm, cm, ssm_state.astype(jnp.float32), chunk)
    y = y + d_skip.astype(jnp.float32)[:, None] * xs
    y = gated_rmsnorm(y.reshape(b, t, D_INNER), z, norm_g).astype(h.dtype)
    return y, qm.reshape(b, t, MEM_HEADS, MEM_HEAD_DIM), new_conv, h_final.astype(h.dtype)


def mem_kv(mem, g, w):
    b, m, _ = mem.shape
    k, v = jnp.split(rms_norm(mem, g) @ w, 2, axis=-1)
    return k.reshape(b, m, MEM_HEADS, MEM_HEAD_DIM), v.reshape(b, m, MEM_HEADS, MEM_HEAD_DIM)


def mem_attend(qm, mk, mv):
    b, t = qm.shape[:2]
    logits = jnp.einsum('bthd,bmhd->bhtm', qm, mk.astype(qm.dtype)).astype(jnp.float32) * MEM_HEAD_DIM ** -0.5
    p = jax.nn.softmax(logits, axis=-1).astype(qm.dtype)
    return jnp.einsum('bhtm,bmhd->bthd', p, mv.astype(qm.dtype)).reshape(b, t, MEM_WIDTH)


def setup_inputs(seed: int = 0) -> dict:
    key = jax.random.key(seed)
    ks = iter(jax.random.split(key, 48))

    def nrm(shape, scale=1.0):
        return jax.random.normal(next(ks), shape, jnp.float32) * scale

    def gain(shape):
        return 1.0 + 0.05 * nrm(shape)

    n_pages = PAST_LEN // PAGE_SIZE
    n_used = DEC_BATCH * n_pages
    n_pool = n_used + n_used // 4
    page_table = jax.random.permutation(next(ks), n_pool)[:n_used].reshape(DEC_BATCH, n_pages).astype(jnp.int32)
    dt0 = jnp.exp(jax.random.uniform(next(ks), (N_SSD_LAYERS, SSM_HEADS)) * (math.log(0.1) - math.log(0.001)) + math.log(0.001))
    dt_bias = dt0 + jnp.log(-jnp.expm1(-dt0))
    a_log = jnp.log(jax.random.uniform(next(ks), (N_SSD_LAYERS, SSM_HEADS), minval=1.0, maxval=16.0))
    return {
        'x_prompt': nrm((BATCH, SEQ, D_MODEL)),
        'x_sample': nrm((DEC_BATCH, DEC_SEQ, D_MODEL)),
        'mem_prompt': nrm((BATCH, MEM_TOKENS, D_MODEL)),
        'cache_k': nrm((N_ATTN_LAYERS, n_pool, PAGE_SIZE, KV_HEADS, HEAD_DIM)),
        'cache_v': nrm((N_ATTN_LAYERS, n_pool, PAGE_SIZE, KV_HEADS, HEAD_DIM)),
        'cache_kidx': nrm((N_ATTN_LAYERS, n_pool, PAGE_SIZE, IDX_DIM)),
        'page_table': page_table,
        'state_ssm': nrm((N_SSD_LAYERS, DEC_BATCH, SSM_HEADS, SSM_HEADDIM, D_STATE), 0.1),
        'state_conv': nrm((N_SSD_LAYERS, DEC_BATCH, CONV_W - 1, CONV_DIM)),
        'cache_mem_k': nrm((DEPTH, DEC_BATCH, MEM_TOKENS, MEM_HEADS, MEM_HEAD_DIM)),
        'cache_mem_v': nrm((DEPTH, DEC_BATCH, MEM_TOKENS, MEM_HEADS, MEM_HEAD_DIM)),
        'rel_bias': nrm((REL_BUCKETS, N_HEADS_A), 0.5),
        'ffn1_g': gain((DEPTH, D_MODEL)),
        'ffn1_w_gu': nrm((DEPTH, D_MODEL, 2 * D_FF), D_MODEL ** -0.5),
        'ffn1_w_down': nrm((DEPTH, D_FF, D_MODEL), D_FF ** -0.5),
        'mix_g': gain((DEPTH, D_MODEL)),
        'mem_g': gain((DEPTH, D_MODEL)),
        'w_mem_kv': nrm((DEPTH, D_MODEL, 2 * MEM_WIDTH), D_MODEL ** -0.5),
        'w_in_attn': nrm((N_ATTN_LAYERS, D_MODEL, ATTN_IN), D_MODEL ** -0.5),
        'w_in_ssd': nrm((N_SSD_LAYERS, D_MODEL, SSM_IN), D_MODEL ** -0.5),
        'conv_w': nrm((N_SSD_LAYERS, CONV_W, CONV_DIM), 0.5),
        'conv_b': nrm((N_SSD_LAYERS, CONV_DIM), 0.01),
        'dt_bias': dt_bias,
        'a_log': a_log,
        'd_skip': gain((N_SSD_LAYERS, SSM_HEADS)),
        'ssd_norm_g': gain((N_SSD_LAYERS, D_INNER)),
        'w_out': nrm((DEPTH, D_INNER + MEM_WIDTH, D_MODEL), (D_INNER + MEM_WIDTH) ** -0.5),
        'ffn2_g': gain((DEPTH, D_MODEL)),
        'ffn2_w_gu': nrm((DEPTH, D_MODEL, 2 * D_FF), D_MODEL ** -0.5),
        'ffn2_w_down': nrm((DEPTH, D_FF, D_MODEL), D_FF ** -0.5),
        'final_g': gain((D_MODEL,)),
    }


def reference(x_prompt, x_sample, mem_prompt, cache_k, cache_v, cache_kidx, page_table, state_ssm, state_conv,
              cache_mem_k, cache_mem_v, rel_bias, ffn1_g, ffn1_w_gu, ffn1_w_down, mix_g, mem_g, w_mem_kv,
              w_in_attn, w_in_ssd, conv_w, conv_b, dt_bias, a_log, d_skip, ssd_norm_g, w_out,
              ffn2_g, ffn2_w_gu, ffn2_w_down, final_g):
    xp, xs = x_prompt, x_sample
    b = xp.shape[0]
    pk, pv, pki, pssm, pconv, pmk, pmv = [], [], [], [], [], [], []
    sk, sv, ski, sssm, sconv = [], [], [], [], []
    for i in range(DEPTH):
        j = i // 2
        xp = xp + 0.5 * swiglu(rms_norm(xp, ffn1_g[i]), ffn1_w_gu[i], ffn1_w_down[i])
        xs = xs + 0.5 * swiglu(rms_norm(xs, ffn1_g[i]), ffn1_w_gu[i], ffn1_w_down[i])
        hp = rms_norm(xp, mix_g[i])
        hs = rms_norm(xs, mix_g[i])
        mkp, mvp = mem_kv(mem_prompt, mem_g[i], w_mem_kv[i])
        pmk.append(mkp)
        pmv.append(mvp)
        if i % 2 == 0:
            mix_p, qm_p, kp, vp, kip = dsa_mixer_prompt(hp, w_in_attn[j], rel_bias)
            mix_s, qm_s, ks_, vs_, kis = dsa_mixer_sample(hs, w_in_attn[j], rel_bias, cache_k[j], cache_v[j],
                                                           cache_kidx[j], page_table)
            pk.append(kp)
            pv.append(vp)
            pki.append(kip)
            sk.append(ks_)
            sv.append(vs_)
            ski.append(kis)
        else:
            conv0 = jnp.zeros((b, CONV_W - 1, CONV_DIM), xp.dtype)
            ssm0 = jnp.zeros((b, SSM_HEADS, SSM_HEADDIM, D_STATE), jnp.float32)
            mix_p, qm_p, cp, stp = ssd_mixer(hp, w_in_ssd[j], conv_w[j], conv_b[j], dt_bias[j], a_log[j],
                                            d_skip[j], ssd_norm_g[j], conv0, ssm0)
            mix_s, qm_s, cs, sts = ssd_mixer(hs, w_in_ssd[j], conv_w[j], conv_b[j], dt_bias[j], a_log[j],
                                            d_skip[j], ssd_norm_g[j], state_conv[j], state_ssm[j])
            pconv.append(cp)
            pssm.append(stp)
            sconv.append(cs)
            sssm.append(sts)
        xp = xp + jnp.concatenate([mix_p, mem_attend(qm_p, mkp, mvp)], axis=-1) @ w_out[i]
        xs = xs + jnp.concatenate([mix_s, mem_attend(qm_s, cache_mem_k[i], cache_mem_v[i])], axis=-1) @ w_out[i]
        xp = xp + 0.5 * swiglu(rms_norm(xp, ffn2_g[i]), ffn2_w_gu[i], ffn2_w_down[i])
        xs = xs + 0.5 * swiglu(rms_norm(xs, ffn2_g[i]), ffn2_w_gu[i], ffn2_w_down[i])
    y_prompt = rms_norm(xp, final_g)
    y_sample = rms_norm(xs, final_g)
    return (y_prompt, y_sample, jnp.stack(pk), jnp.stack(pv), jnp.stack(pki), jnp.stack(pssm), jnp.stack(pconv),
            jnp.stack(pmk), jnp.stack(pmv), jnp.stack(sk), jnp.stack(sv), jnp.stack(ski), jnp.stack(sssm),
            jnp.stack(sconv))
```

```python
import functools
import math

import numpy as np
import jax
import jax.numpy as jnp
from jax import lax
from jax.experimental import pallas as pl
from jax.experimental.pallas import tpu as pltpu

F32, BF16, I32 = jnp.float32, jnp.bfloat16, jnp.int32
HI = lax.Precision.HIGHEST
NT_DIMS = (((1,), (1,)), ((), ()))

HEAD_DIM = 128
KV_HEADS = 4
Q_PER_KV = 3
N_HEADS_A = KV_HEADS * Q_PER_KV
IDX_HEADS = 16
IDX_DIM = 64
TOPK_MAX = 256
REL_BUCKETS = 32
REL_MAX_EXACT = 16
REL_MAX_DIST = 128
MEM_HEADS = 4
MEM_HEAD_DIM = 128
MEM_WIDTH = MEM_HEADS * MEM_HEAD_DIM
SSM_HEADDIM = 64
SSM_GROUPS = 4
D_STATE = 128
CONV_W = 4
PAGE_SIZE = 128
EPS = 1e-6

LANES = 128
SUBLANES = 8
NEG = -1e30
INT_MIN = -2 ** 31
IDX_SCALE = IDX_DIM ** -0.5 * IDX_HEADS ** -0.5
ATT_SCALE = HEAD_DIM ** -0.5
MEM_SCALE = MEM_HEAD_DIM ** -0.5

QB = 128
KC = 512
TM = 512
TF = 512
SSD_L = 128
SEQ_PAD = 8


def _cparams(sem, vmem_mb):
    return pltpu.CompilerParams(dimension_semantics=sem, vmem_limit_bytes=vmem_mb * 2 ** 20)


def _round_up(n, m):
    return (n + m - 1) // m * m


def _const_spec(shape):
    nd = len(shape)
    return pl.BlockSpec(shape, lambda *_: (0,) * nd)


def _resident_spec(shape):
    nd = len(shape)
    return pl.BlockSpec(shape, lambda *_: (0,) * nd, pipeline_mode=pl.Buffered(1))


def _rms(x, g):
    return x * lax.rsqrt(jnp.mean(x * x, axis=-1, keepdims=True) + EPS) * g


def _sigmoid(x):
    return 1.0 / (1.0 + jnp.exp(-x))


def _softplus(x):
    return jnp.maximum(x, 0.0) + jnp.log1p(jnp.exp(-jnp.abs(x)))


def _ffn_body(x_ref, g_ref, wgu_ref, wd_ref, *rest, tf, nj, with_final):
    if with_final:
        fg_ref, o_ref, y_ref, xn_ref, acc_ref = rest
    else:
        o_ref, xn_ref, acc_ref = rest
    j = pl.program_id(1)

    @pl.when(j == 0)
    def _():
        xn_ref[...] = _rms(x_ref[...], g_ref[...]).astype(BF16)
        acc_ref[...] = jnp.zeros_like(acc_ref)

    h = jnp.dot(xn_ref[...], wgu_ref[...], preferred_element_type=F32)
    gate, up = h[:, :tf], h[:, tf:]
    a = gate * _sigmoid(gate) * up
    acc_ref[...] += jnp.dot(a.astype(BF16), wd_ref[...], preferred_element_type=F32)

    @pl.when(j == nj - 1)
    def _():
        o = x_ref[...] + 0.5 * acc_ref[...]
        o_ref[...] = o
        if with_final:
            y_ref[...] = _rms(o, fg_ref[...])


def _prep_ffn(w_gu, w_d, tf):
    d, two_ff = w_gu.shape
    ff = two_ff // 2
    ffp = _round_up(ff, tf)
    nj = ffp // tf
    pad = ((0, 0), (0, ffp - ff))
    wg = jnp.pad(w_gu[:, :ff].astype(BF16), pad).reshape(d, nj, tf)
    wu = jnp.pad(w_gu[:, ff:].astype(BF16), pad).reshape(d, nj, tf)
    wgu = jnp.concatenate([wg, wu], axis=2).reshape(d, nj * 2 * tf)
    wd = jnp.pad(w_d.astype(BF16), ((0, ffp - ff), (0, 0)))
    return wgu, wd, nj


def ffn(x, g, w_gu, w_d, final_g=None, tm=TM, tf=TF):
    t, d = x.shape
    wgu, wd, nj = _prep_ffn(w_gu, w_d, tf)
    with_final = final_g is not None
    in_specs = [
        pl.BlockSpec((tm, d), lambda i, j: (i, 0)),
        _const_spec((1, d)),
        pl.BlockSpec((d, 2 * tf), lambda i, j: (0, j)),
        pl.BlockSpec((tf, d), lambda i, j: (j, 0)),
    ]
    args = [x, g.reshape(1, d), wgu, wd]
    out_shape = [jax.ShapeDtypeStruct((t, d), F32)]
    out_specs = [pl.BlockSpec((tm, d), lambda i, j: (i, 0))]
    if with_final:
        in_specs.append(_const_spec((1, d)))
        args.append(final_g.reshape(1, d))
        out_shape.append(jax.ShapeDtypeStruct((t, d), F32))
        out_specs.append(pl.BlockSpec((tm, d), lambda i, j: (i, 0)))
    res = pl.pallas_call(
        functools.partial(_ffn_body, tf=tf, nj=nj, with_final=with_final),
        grid=(t // tm, nj),
        in_specs=in_specs,
        out_specs=out_specs,
        out_shape=out_shape,
        scratch_shapes=[pltpu.VMEM((tm, d), BF16), pltpu.VMEM((tm, d), F32)],
        compiler_params=_cparams(("arbitrary", "arbitrary"), 56),
        name="ffn",
    )(*args)
    return res if with_final else res[0]


def _norm_matmul_body(x_ref, g_ref, w_ref, o_ref, xn_ref):
    @pl.when(pl.program_id(1) == 0)
    def _():
        xn_ref[...] = _rms(x_ref[...], g_ref[...]).astype(BF16)

    o_ref[...] = jnp.dot(xn_ref[...], w_ref[...], preferred_element_type=F32)


def norm_matmul(x, g, w_bf16, tm, tn):
    t, d = x.shape
    n = w_bf16.shape[1]
    return pl.pallas_call(
        _norm_matmul_body,
        grid=(t // tm, n // tn),
        in_specs=[
            pl.BlockSpec((tm, d), lambda i, j: (i, 0)),
            _const_spec((1, d)),
            pl.BlockSpec((d, tn), lambda i, j: (0, j)),
        ],
        out_specs=pl.BlockSpec((tm, tn), lambda i, j: (i, j)),
        out_shape=jax.ShapeDtypeStruct((t, n), F32),
        scratch_shapes=[pltpu.VMEM((tm, d), BF16)],
        compiler_params=_cparams(("arbitrary", "arbitrary"), 48),
        name="norm_matmul",
    )(x, g.reshape(1, d), w_bf16)


def _out_proj_body(x_ref, mix_ref, mem_ref, w1_ref, w2_ref, o_ref):
    o_ref[...] = (x_ref[...]
                  + jnp.dot(mix_ref[...], w1_ref[...], preferred_element_type=F32)
                  + jnp.dot(mem_ref[...], w2_ref[...], preferred_element_type=F32))


def out_proj(x, mix, mem, w_out, tm=TM):
    t, d = x.shape
    dm, dw = mix.shape[1], mem.shape[1]
    w1 = w_out[:dm].astype(BF16)
    w2 = w_out[dm:].astype(BF16)
    return pl.pallas_call(
        _out_proj_body,
        grid=(t // tm,),
        in_specs=[
            pl.BlockSpec((tm, d), lambda i: (i, 0)),
            pl.BlockSpec((tm, dm), lambda i: (i, 0)),
            pl.BlockSpec((tm, dw), lambda i: (i, 0)),
            _const_spec((dm, d)),
            _const_spec((dw, d)),
        ],
        out_specs=pl.BlockSpec((tm, d), lambda i: (i, 0)),
        out_shape=jax.ShapeDtypeStruct((t, d), F32),
        compiler_params=_cparams(("arbitrary",), 48),
        name="out_proj",
    )(x, mix, mem, w1, w2)


def _softmax_rows(s):
    m = jnp.max(s, axis=-1, keepdims=True)
    p = jnp.exp(s - m)
    return p, jnp.sum(p, axis=-1, keepdims=True)


def _mem_attn_prompt_body(q_ref, mk_ref, mv_ref, o_ref):
    for h in range(MEM_HEADS):
        sl = slice(h * MEM_HEAD_DIM, (h + 1) * MEM_HEAD_DIM)
        q = (q_ref[:, sl] * MEM_SCALE).astype(BF16)
        s = lax.dot_general(q, mk_ref[:, sl].astype(BF16), NT_DIMS, preferred_element_type=F32)
        p, l = _softmax_rows(s)
        o = jnp.dot(p.astype(BF16), mv_ref[:, sl].astype(BF16), preferred_element_type=F32)
        o_ref[:, sl] = (o / l).astype(BF16)


def mem_attn_prompt(z, qm_col_block, n_rows, mkv, tm=TM):
    m = mkv.shape[0]
    return pl.pallas_call(
        _mem_attn_prompt_body,
        grid=(n_rows // tm,),
        in_specs=[
            pl.BlockSpec((tm, MEM_WIDTH), lambda i: (i, qm_col_block)),
            pl.BlockSpec((m, MEM_WIDTH), lambda i: (0, 0)),
            pl.BlockSpec((m, MEM_WIDTH), lambda i: (0, 1)),
        ],
        out_specs=pl.BlockSpec((tm, MEM_WIDTH), lambda i: (i, 0)),
        out_shape=jax.ShapeDtypeStruct((n_rows, MEM_WIDTH), BF16),
        compiler_params=_cparams(("arbitrary",), 32),
        name="mem_attn_prompt",
    )(z, mkv, mkv)


def _mem_attn_sample_body(q_ref, k_ref, v_ref, o_ref, *, bb):
    rows = MEM_HEADS * SEQ_PAD
    rh = lax.broadcasted_iota(I32, (rows, MEM_WIDTH), 0)
    ch = lax.broadcasted_iota(I32, (rows, MEM_WIDTH), 1)
    own = None
    for h in range(MEM_HEADS):
        t = ((rh >= h * SEQ_PAD) & (rh < (h + 1) * SEQ_PAD)
             & (ch >= h * MEM_HEAD_DIM) & (ch < (h + 1) * MEM_HEAD_DIM))
        own = t if own is None else (own | t)
    for b in range(bb):
        q8 = q_ref[b] * MEM_SCALE
        qbd = jnp.where(own, jnp.concatenate([q8] * MEM_HEADS, axis=0), 0.0).astype(BF16)
        s = lax.dot_general(qbd, k_ref[b].astype(BF16), NT_DIMS, preferred_element_type=F32)
        p, l = _softmax_rows(s)
        r = jnp.dot(p.astype(BF16), v_ref[b].astype(BF16), preferred_element_type=F32) / l
        r = jnp.where(own, r, 0.0)
        o = r[0:SEQ_PAD]
        for h in range(1, MEM_HEADS):
            o = o + r[h * SEQ_PAD:(h + 1) * SEQ_PAD]
        o_ref[b] = o.astype(BF16)


def mem_attn_sample(qm8, mem_k, mem_v, bb=8):
    b, m, _ = mem_k.shape
    return pl.pallas_call(
        functools.partial(_mem_attn_sample_body, bb=bb),
        grid=(b // bb,),
        in_specs=[
            pl.BlockSpec((bb, SEQ_PAD, MEM_WIDTH), lambda i: (i, 0, 0)),
            pl.BlockSpec((bb, m, MEM_WIDTH), lambda i: (i, 0, 0)),
            pl.BlockSpec((bb, m, MEM_WIDTH), lambda i: (i, 0, 0)),
        ],
        out_specs=pl.BlockSpec((bb, SEQ_PAD, MEM_WIDTH), lambda i: (i, 0, 0)),
        out_shape=jax.ShapeDtypeStruct((b, SEQ_PAD, MEM_WIDTH), BF16),
        compiler_params=_cparams(("arbitrary",), 40),
        name="mem_attn_sample",
    )(qm8, mem_k, mem_v)


def _sortable_key(x):
    b = pltpu.bitcast(x, I32)
    return jnp.where(b < 0, (b ^ 0x7FFFFFFF) + 1, b)


def _t5_bucket_np(dist):
    n = np.maximum(dist, 0)
    nf = np.maximum(n, 1).astype(np.float64)
    large = REL_MAX_EXACT + (np.log(nf / REL_MAX_EXACT) / math.log(REL_MAX_DIST / REL_MAX_EXACT)
                             * (REL_BUCKETS - REL_MAX_EXACT)).astype(np.int32)
    large = np.minimum(large, REL_BUCKETS - 1)
    return np.where(n < REL_MAX_EXACT, n, large)


FAR_DIST = int(np.min(np.nonzero(_t5_bucket_np(np.arange(4 * REL_MAX_DIST)) == REL_BUCKETS - 1)[0]))
assert np.all(_t5_bucket_np(np.arange(FAR_DIST, 1 << 16)) == REL_BUCKETS - 1) and FAR_DIST <= QB


def _shifted_bias(rel_bias, dist):
    dist = np.asarray(dist)
    b = _t5_bucket_np(dist)
    vals = rel_bias[b] - rel_bias[REL_BUCKETS - 1]
    keep = (dist >= 0) & (b != REL_BUCKETS - 1)
    return jnp.where(keep[..., None], vals, 0.0)


def _threshold_search(count_ge, count_tie_lt, shape, topk, n_idx_bits, all_idx):
    zero = jnp.zeros(shape, I32)
    t0 = jnp.where(count_ge(zero) >= topk, zero, jnp.full(shape, INT_MIN, I32))

    def bit_body(b, t):
        cand = t | lax.shift_left(jnp.int32(1), 30 - b)
        return jnp.where(count_ge(cand) >= topk, cand, t)

    t = lax.fori_loop(0, 31, bit_body, t0)
    has_k = t > INT_MIN
    n_ge = count_ge(t)
    need = topk - count_ge(t + 1)
    excess = jnp.max(jnp.where(has_k, n_ge, 0.0)) > topk

    def tie_search(_):
        def jbit(b, j):
            cand = j | lax.shift_left(jnp.int32(1), n_idx_bits - 1 - b)
            return jnp.where(count_tie_lt(t, cand) < need, cand, j)
        return lax.fori_loop(0, n_idx_bits, jbit, jnp.zeros(shape, I32))

    j = lax.cond(excess, tie_search, lambda _: jnp.full(shape, all_idx, I32), 0)
    j = jnp.where(has_k, j, -1)
    return t, j


def _dsa_prompt_body(q_ref, qi_ref, kw_ref, kta_ref, ktb_ref, k_ref, v_ref, bias_ref, o_ref,
                     lhs_ref, wb_ref, key_ref, qs_ref, t_ref, j_ref, m_ref, l_ref, acc_ref,
                     *, seq, topk):
    i = pl.program_id(0)
    q0 = i * QB
    n_pairs = IDX_HEADS // 2
    rep = KC // LANES

    for p in range(n_pairs):
        lhs_ref[p * QB:(p + 1) * QB, :] = qi_ref[:, p * LANES:(p + 1) * LANES].astype(BF16)
    kw = kw_ref[...]
    for h in range(IDX_HEADS):
        wb_ref[h] = jnp.broadcast_to(kw[:, IDX_DIM + h:IDX_DIM + h + 1] * IDX_SCALE, (QB, LANES))

    n_chunks = (jnp.maximum(i + 1, 2) * QB + KC - 1) // KC

    def score_chunk(c, carry):
        c0 = pl.multiple_of(c * KC, KC)
        lhs = lhs_ref[...]
        xa = jnp.dot(lhs, kta_ref[:, pl.ds(c0, KC)], preferred_element_type=F32)
        xb = jnp.dot(lhs, ktb_ref[:, pl.ds(c0, KC)], preferred_element_type=F32)
        acc = jnp.zeros((QB, KC), F32)
        for p in range(n_pairs):
            wa = jnp.concatenate([wb_ref[2 * p]] * rep, axis=1)
            wb = jnp.concatenate([wb_ref[2 * p + 1]] * rep, axis=1)
            acc = acc + jnp.maximum(xa[p * QB:(p + 1) * QB], 0.0) * wa
            acc = acc + jnp.maximum(xb[p * QB:(p + 1) * QB], 0.0) * wb
        kpos = c0 + lax.broadcasted_iota(I32, (QB, KC), 1)
        qpos = q0 + lax.broadcasted_iota(I32, (QB, KC), 0)
        key_ref[:, pl.ds(c0, KC)] = jnp.where(kpos <= qpos, _sortable_key(acc), INT_MIN)
        return carry

    lax.fori_loop(0, n_chunks, score_chunk, 0)

    def count_ge(cand):
        cb = jnp.broadcast_to(cand, (QB, LANES))

        def body(c, a):
            kk = key_ref[:, pl.ds(pl.multiple_of(c * KC, KC), KC)]
            for u in range(rep):
                a = a + jnp.where(kk[:, u * LANES:(u + 1) * LANES] >= cb, 1.0, 0.0)
            return a

        a = lax.fori_loop(0, n_chunks, body, jnp.zeros((QB, LANES), F32))
        return jnp.sum(a, axis=1, keepdims=True)

    def count_tie_lt(t, jc):
        tb = jnp.broadcast_to(t, (QB, LANES))
        jb = jnp.broadcast_to(jc, (QB, LANES))
        lane = lax.broadcasted_iota(I32, (QB, LANES), 1)

        def body(c, a):
            c0 = pl.multiple_of(c * KC, KC)
            kk = key_ref[:, pl.ds(c0, KC)]
            for u in range(rep):
                hit = (kk[:, u * LANES:(u + 1) * LANES] == tb) & (c0 + u * LANES + lane < jb)
                a = a + jnp.where(hit, 1.0, 0.0)
            return a

        a = lax.fori_loop(0, n_chunks, body, jnp.zeros((QB, LANES), F32))
        return jnp.sum(a, axis=1, keepdims=True)

    t, j = _threshold_search(count_ge, count_tie_lt, (QB, 1), topk, (seq - 1).bit_length(), seq)
    t_ref[...] = jnp.broadcast_to(t, (QB, LANES))
    j_ref[...] = jnp.broadcast_to(j, (QB, LANES))

    for g in range(KV_HEADS):
        for r in range(Q_PER_KV):
            hq = g * Q_PER_KV + r
            qs_ref[g, r * QB:(r + 1) * QB, :] = (
                q_ref[:, hq * HEAD_DIM:(hq + 1) * HEAD_DIM] * ATT_SCALE).astype(BF16)
    m_ref[...] = jnp.full(m_ref.shape, NEG, F32)
    l_ref[...] = jnp.zeros(l_ref.shape, F32)
    acc_ref[...] = jnp.zeros(acc_ref.shape, F32)

    def attn_chunk(c0, width, far_end, bias_of_group):
        kk = key_ref[:, pl.ds(c0, width)]
        kpos = c0 + lax.broadcasted_iota(I32, (QB, width), 1)
        tb = jnp.concatenate([t_ref[...]] * (width // LANES), axis=1)
        jb = jnp.concatenate([j_ref[...]] * (width // LANES), axis=1)
        sel = (kk > tb) | ((kk == tb) & (kpos <= jb))
        if far_end is not None:
            sel = sel & (kpos < far_end)
        madd = jnp.where(sel, 0.0, NEG)
        madd = jnp.concatenate([madd] * Q_PER_KV, axis=0)
        for g in range(KV_HEADS):
            gs = slice(g * HEAD_DIM, (g + 1) * HEAD_DIM)
            s = lax.dot_general(qs_ref[g], k_ref[pl.ds(c0, width), gs], NT_DIMS,
                                preferred_element_type=F32)
            if bias_of_group is not None:
                s = s + bias_of_group(g)
            s = s + madd
            m_old = m_ref[g]
            m_new = jnp.maximum(m_old, jnp.max(s, axis=1, keepdims=True))
            alpha = jnp.exp(m_old - m_new)
            p = jnp.exp(s - m_new[:, 0:1])
            l_ref[g] = alpha * l_ref[g] + jnp.sum(p, axis=1, keepdims=True)
            acc_ref[g] = alpha * acc_ref[g] + jnp.dot(
                p.astype(BF16), v_ref[pl.ds(c0, width), gs], preferred_element_type=F32)
            m_ref[g] = m_new

    far_end = jnp.maximum(i - 1, 0) * QB

    def far_body(c, carry):
        attn_chunk(pl.multiple_of(c * KC, KC), KC, far_end, None)
        return carry

    lax.fori_loop(0, (far_end + KC - 1) // KC, far_body, 0)
    variant = jnp.minimum(i, 1)
    attn_chunk(pl.multiple_of(far_end, QB), 2 * QB, None, lambda g: bias_ref[variant, g])

    for g in range(KV_HEADS):
        o = acc_ref[g] / l_ref[g]
        for r in range(Q_PER_KV):
            hq = g * Q_PER_KV + r
            o_ref[:, hq * HEAD_DIM:(hq + 1) * HEAD_DIM] = o[r * QB:(r + 1) * QB].astype(BF16)


def _prompt_bias(rel_bias):
    t = np.arange(QB)[:, None]
    c = np.arange(2 * QB)[None, :]
    out = []
    for dist in (t - c, QB + t - c):
        b = _shifted_bias(rel_bias, dist)
        b = b.reshape(QB, 2 * QB, KV_HEADS, Q_PER_KV).transpose(2, 3, 0, 1)
        out.append(b.reshape(KV_HEADS, Q_PER_KV * QB, 2 * QB))
    return jnp.stack(out).astype(F32)


def dsa_prompt(z, seq, col, kta, ktb, k_bf, v_bf, rel_bias):
    topk = min(TOPK_MAX, seq // 4)
    assert seq % KC == 0 and seq >= 2 * QB
    qw = N_HEADS_A * HEAD_DIM
    qiw = IDX_HEADS * IDX_DIM
    bias = _prompt_bias(rel_bias)
    rows = Q_PER_KV * QB
    return pl.pallas_call(
        functools.partial(_dsa_prompt_body, seq=seq, topk=topk),
        grid=(seq // QB,),
        in_specs=[
            pl.BlockSpec((QB, qw), lambda i: (i, col["q"] // qw)),
            pl.BlockSpec((QB, qiw), lambda i: (i, col["qi"] // qiw)),
            pl.BlockSpec((QB, LANES), lambda i: (i, col["kw"] // LANES)),
            _resident_spec((LANES, seq)),
            _resident_spec((LANES, seq)),
            _resident_spec((seq, KV_HEADS * HEAD_DIM)),
            _resident_spec((seq, KV_HEADS * HEAD_DIM)),
            _resident_spec(bias.shape),
        ],
        out_specs=pl.BlockSpec((QB, qw), lambda i: (i, 0)),
        out_shape=jax.ShapeDtypeStruct((seq, qw), BF16),
        scratch_shapes=[
            pltpu.VMEM((IDX_HEADS // 2 * QB, LANES), BF16),
            pltpu.VMEM((IDX_HEADS, QB, LANES), F32),
            pltpu.VMEM((QB, seq), I32),
            pltpu.VMEM((KV_HEADS, rows, HEAD_DIM), BF16),
            pltpu.VMEM((QB, LANES), I32),
            pltpu.VMEM((QB, LANES), I32),
            pltpu.VMEM((KV_HEADS, rows, LANES), F32),
            pltpu.VMEM((KV_HEADS, rows, LANES), F32),
            pltpu.VMEM((KV_HEADS, rows, HEAD_DIM), F32),
        ],
        compiler_params=_cparams(("arbitrary",), 56),
        name="dsa_prompt",
    )(z, z, z, kta, ktb, k_bf, v_bf, bias)


def _dsa_sample_body(pt_ref, qi_ref, w_ref, q_ref, bias_ref, *refs, n_pages, t_valid, topk):
    del pt_ref
    np1 = n_pages + 1
    kidx_refs, k_refs, v_refs = refs[0:np1], refs[np1:2 * np1], refs[2 * np1:3 * np1]
    o_ref, key_ref, s_ref = refs[3 * np1:]
    past = n_pages * PAGE_SIZE
    width = np1 * PAGE_SIZE
    rows = KV_HEADS * Q_PER_KV * SEQ_PAD

    qi = qi_ref[...].astype(BF16)
    w = w_ref[...] * IDX_SCALE
    for p in range(np1):
        x = lax.dot_general(qi, kidx_refs[p][...].astype(BF16), NT_DIMS, preferred_element_type=F32)
        sc = (jnp.maximum(x, 0.0) * w).reshape(SEQ_PAD, IDX_HEADS, PAGE_SIZE).sum(axis=1)
        kpos = p * PAGE_SIZE + lax.broadcasted_iota(I32, (SEQ_PAD, PAGE_SIZE), 1)
        qpos = past + lax.broadcasted_iota(I32, (SEQ_PAD, PAGE_SIZE), 0)
        ok = (kpos <= qpos) & (kpos < past + t_valid)
        key_ref[:, p * PAGE_SIZE:(p + 1) * PAGE_SIZE] = jnp.where(ok, _sortable_key(sc), INT_MIN)

    keys = key_ref[...]
    kpos = lax.broadcasted_iota(I32, (SEQ_PAD, width), 1)

    def count_ge(cand):
        return jnp.sum(jnp.where(keys >= cand, 1.0, 0.0), axis=1, keepdims=True)

    def count_tie_lt(t, jc):
        return jnp.sum(jnp.where((keys == t) & (kpos < jc), 1.0, 0.0), axis=1, keepdims=True)

    t, j = _threshold_search(count_ge, count_tie_lt, (SEQ_PAD, 1), topk, (width - 1).bit_length(), width)
    sel = (keys > t) | ((keys == t) & (kpos <= j))
    madd = jnp.where(sel, 0.0, NEG)
    madd = jnp.concatenate([madd] * (KV_HEADS * Q_PER_KV), axis=0)

    rg = lax.broadcasted_iota(I32, (rows, KV_HEADS * HEAD_DIM), 0)
    cg = lax.broadcasted_iota(I32, (rows, KV_HEADS * HEAD_DIM), 1)
    grp_rows = Q_PER_KV * SEQ_PAD
    own = None
    for g in range(KV_HEADS):
        tsel = ((rg >= g * grp_rows) & (rg < (g + 1) * grp_rows)
                & (cg >= g * HEAD_DIM) & (cg < (g + 1) * HEAD_DIM))
        own = tsel if own is None else (own | tsel)
    qc = q_ref[...] * ATT_SCALE
    qbd = jnp.where(own, jnp.concatenate([qc] * KV_HEADS, axis=1), 0.0).astype(BF16)
    for p in range(np1):
        s_ref[:, p * PAGE_SIZE:(p + 1) * PAGE_SIZE] = lax.dot_general(
            qbd, k_refs[p][...].astype(BF16), NT_DIMS, preferred_element_type=F32)
    pr, l = _softmax_rows(s_ref[...] + bias_ref[...] + madd)
    acc = jnp.zeros((rows, KV_HEADS * HEAD_DIM), F32)
    for p in range(np1):
        acc = acc + jnp.dot(pr[:, p * PAGE_SIZE:(p + 1) * PAGE_SIZE].astype(BF16),
                            v_refs[p][...].astype(BF16), preferred_element_type=F32)
    acc = jnp.where(own, acc / l, 0.0)
    o = acc[:, 0:HEAD_DIM]
    for g in range(1, KV_HEADS):
        o = o + acc[:, g * HEAD_DIM:(g + 1) * HEAD_DIM]
    o_ref[...] = o


def _sample_bias(rel_bias, past, width):
    t = np.arange(SEQ_PAD)[:, None]
    s = np.arange(width)[None, :]
    b = _shifted_bias(rel_bias, past + t - s)
    return b.transpose(2, 0, 1).reshape(N_HEADS_A * SEQ_PAD, width).astype(F32)


def dsa_sample(qi8, w8, q8, kidx_new, k_new, v_new, cache_kidx, cache_k, cache_v, page_table,
               rel_bias, t_valid):
    b, n_pages = page_table.shape
    np1 = n_pages + 1
    past = n_pages * PAGE_SIZE
    width = np1 * PAGE_SIZE
    topk = min(TOPK_MAX, (past + t_valid) // 4)
    bias = _sample_bias(rel_bias, past, width)
    rows = N_HEADS_A * SEQ_PAD
    kvw = KV_HEADS * HEAD_DIM

    def page_spec(last, p):
        return pl.BlockSpec((None, PAGE_SIZE, last), lambda i, pt: (pt[i, p], 0, 0))

    def new_spec(last):
        return pl.BlockSpec((None, PAGE_SIZE, last), lambda i, pt: (i, 0, 0))

    in_specs = [
        pl.BlockSpec((None, SEQ_PAD * IDX_HEADS, IDX_DIM), lambda i, pt: (i, 0, 0)),
        pl.BlockSpec((None, SEQ_PAD * IDX_HEADS, 1), lambda i, pt: (i, 0, 0)),
        pl.BlockSpec((None, rows, HEAD_DIM), lambda i, pt: (i, 0, 0)),
        pl.BlockSpec((rows, width), lambda i, pt: (0, 0)),
    ]
    args = [qi8, w8, q8, bias]
    for arr_cache, arr_new, last in ((cache_kidx, kidx_new, IDX_DIM), (cache_k, k_new, kvw), (cache_v, v_new, kvw)):
        in_specs += [page_spec(last, p) for p in range(n_pages)] + [new_spec(last)]
        args += [arr_cache] * n_pages + [arr_new]
    return pl.pallas_call(
        functools.partial(_dsa_sample_body, n_pages=n_pages, t_valid=t_valid, topk=topk),
        grid_spec=pltpu.PrefetchScalarGridSpec(
            num_scalar_prefetch=1,
            grid=(b,),
            in_specs=in_specs,
            out_specs=pl.BlockSpec((None, rows, HEAD_DIM), lambda i, pt: (i, 0, 0)),
            scratch_shapes=[pltpu.VMEM((SEQ_PAD, width), I32), pltpu.VMEM((rows, width), F32)],
        ),
        out_shape=jax.ShapeDtypeStruct((b, rows, HEAD_DIM), F32),
        compiler_params=_cparams(("arbitrary",), 48),
        name="dsa_sample",
    )(page_table, *args)


def _conv_body(x_ref, st_ref, w_ref, b_ref, o_ref, tail_ref, *, rows):
    @pl.when(pl.program_id(1) == 0)
    def _():
        tail_ref[...] = st_ref[...]

    x = x_ref[...]
    xc = jnp.concatenate([tail_ref[...], x], axis=0)
    out = b_ref[...] + x * w_ref[CONV_W - 1:CONV_W, :]
    for k in range(1, CONV_W):
        shifted = pltpu.roll(xc, k, 0)[SUBLANES:SUBLANES + rows]
        out = out + shifted * w_ref[CONV_W - 1 - k:CONV_W - k, :]
    o_ref[...] = out * _sigmoid(out)
    tail_ref[...] = x[rows - SUBLANES:rows]


def conv_silu(z3, col_block, state8, conv_w, conv_b, rows, cdim):
    b, t, _ = z3.shape
    return pl.pallas_call(
        functools.partial(_conv_body, rows=rows),
        grid=(b, t // rows),
        in_specs=[
            pl.BlockSpec((None, rows, cdim), lambda i, c: (i, c, col_block)),
            pl.BlockSpec((None, SUBLANES, cdim), lambda i, c: (i, 0, 0)),
            _const_spec((CONV_W, cdim)),
            _const_spec((1, cdim)),
        ],
        out_specs=pl.BlockSpec((None, rows, cdim), lambda i, c: (i, c, 0)),
        out_shape=jax.ShapeDtypeStruct((b, t, cdim), F32),
        scratch_shapes=[pltpu.VMEM((SUBLANES, cdim), F32)],
        compiler_params=_cparams(("arbitrary", "arbitrary"), 32),
        name="conv_silu",
    )(z3, state8, conv_w, conv_b.reshape(1, cdim))


def _ssd_pair(xs, dt_raw, dtb, a, dskip, bm, cm, cb, lmask, valid):
    dt = _softplus(dt_raw + dtb)
    if valid is not None:
        dt = jnp.where(valid, dt, 0.0)
    xdt = xs * dt
    acs = jnp.dot(jnp.where(lmask, 1.0, 0.0), dt * a, precision=HI, preferred_element_type=F32)
    acs_t = acs.T
    half = SSM_HEADDIM
    yd = []
    for lane0 in (0, half):
        seg = acs[:, lane0:lane0 + 1] - acs_t[lane0:lane0 + 1, :]
        lm = jnp.where(lmask, jnp.exp(jnp.where(lmask, seg, 0.0)), 0.0)
        yd.append(jnp.dot(cb * lm, xdt, precision=HI, preferred_element_type=F32))
    lane = lax.broadcasted_iota(I32, xs.shape, 1)
    y = jnp.where(lane < half, yd[0], yd[1]) + dskip * xs
    return xdt, acs, y


def _ssd_prompt_body(xs_ref, bm_ref, cm_ref, dt_ref, dtb_ref, a_ref, dsk_ref, y_ref, hf_ref, st_ref,
                     *, n_pairs, n_chunks):
    c = pl.program_id(0)

    @pl.when(c == 0)
    def _():
        st_ref[...] = jnp.zeros_like(st_ref)

    ll = SSD_L
    li = lax.broadcasted_iota(I32, (ll, ll), 0)
    si = lax.broadcasted_iota(I32, (ll, ll), 1)
    lmask = si <= li
    pairs_per_group = n_pairs // SSM_GROUPS
    for g in range(SSM_GROUPS):
        gs = slice(g * D_STATE, (g + 1) * D_STATE)
        bm = bm_ref[:, gs]
        cm = cm_ref[:, gs]
        cb = lax.dot_general(cm, bm, NT_DIMS, precision=HI, preferred_element_type=F32)
        for kk in range(pairs_per_group):
            k = g * pairs_per_group + kk
            ks = slice(k * LANES, (k + 1) * LANES)
            xdt, acs, y = _ssd_pair(xs_ref[:, ks], dt_ref[:, ks], dtb_ref[:, ks], a_ref[:, ks],
                                    dsk_ref[:, ks], bm, cm, cb, lmask, None)
            acs_last = acs[ll - 1:ll, :]
            state = st_ref[k]
            y_off = lax.dot_general(cm, state, NT_DIMS, precision=HI, preferred_element_type=F32)
            y_ref[:, ks] = y + y_off * jnp.exp(acs)
            xd_t = (xdt * jnp.exp(acs_last - acs)).T
            upd = jnp.dot(xd_t, bm, precision=HI, preferred_element_type=F32)
            cd = jnp.exp(jnp.broadcast_to(acs_last, (ll, LANES))).T[:, 0:1]
            st_ref[k] = state * cd + upd

    @pl.when(c == n_chunks - 1)
    def _():
        hf_ref[...] = st_ref[...]


def ssd_prompt(xc, dt_exp, dtb_exp, a_exp, dsk_exp, d_inner):
    t = xc.shape[0]
    n_pairs = d_inner // LANES
    gn = SSM_GROUPS * D_STATE
    n_chunks = t // SSD_L
    y, hf = pl.pallas_call(
        functools.partial(_ssd_prompt_body, n_pairs=n_pairs, n_chunks=n_chunks),
        grid=(n_chunks,),
        in_specs=[
            pl.BlockSpec((SSD_L, d_inner), lambda c: (c, 0)),
            pl.BlockSpec((SSD_L, gn), lambda c: (c, d_inner // gn)),
            pl.BlockSpec((SSD_L, gn), lambda c: (c, d_inner // gn + 1)),
            pl.BlockSpec((SSD_L, d_inner), lambda c: (c, 0)),
            _const_spec((1, d_inner)),
            _const_spec((1, d_inner)),
            _const_spec((1, d_inner)),
        ],
        out_specs=[
            pl.BlockSpec((SSD_L, d_inner), lambda c: (c, 0)),
            _const_spec((n_pairs, LANES, D_STATE)),
        ],
        out_shape=[
            jax.ShapeDtypeStruct((t, d_inner), F32),
            jax.ShapeDtypeStruct((n_pairs, LANES, D_STATE), F32),
        ],
        scratch_shapes=[pltpu.VMEM((n_pairs, LANES, D_STATE), F32)],
        compiler_params=_cparams(("arbitrary",), 32),
        name="ssd_prompt",
    )(xc, xc, xc, dt_exp, dtb_exp, a_exp, dsk_exp)
    return y, hf.reshape(2 * n_pairs, SSM_HEADDIM, D_STATE)


def _ssd_sample_body(xs_ref, bm_ref, cm_ref, dt_ref, dtb_ref, a_ref, dsk_ref, h0_ref, y_ref, h1_ref,
                     *, n_seq, t_valid):
    ll = n_seq * SEQ_PAD
    li = lax.broadcasted_iota(I32, (ll, ll), 0)
    si = lax.broadcasted_iota(I32, (ll, ll), 1)
    same = lax.shift_right_logical(li, 3) == lax.shift_right_logical(si, 3)
    lmask = same & (si <= li)
    last = same & ((si & (SEQ_PAD - 1)) == SEQ_PAD - 1)
    rowi = lax.broadcasted_iota(I32, (ll, LANES), 0)
    valid = (rowi & (SEQ_PAD - 1)) < t_valid
    bm = bm_ref[...]
    cm = cm_ref[...]
    cb = lax.dot_general(cm, bm, NT_DIMS, precision=HI, preferred_element_type=F32)
    xdt, acs, y = _ssd_pair(xs_ref[...], dt_ref[...], dtb_ref[...], a_ref[...], dsk_ref[...],
                            bm, cm, cb, lmask, valid)
    acs_last = jnp.dot(jnp.where(last, 1.0, 0.0), acs, precision=HI, preferred_element_type=F32)
    e_acs = jnp.exp(acs)
    xd_t = (xdt * jnp.exp(acs_last - acs)).T
    cd_t = jnp.exp(acs_last).T
    lane = lax.broadcasted_iota(I32, (LANES, ll), 1)
    y_off = []
    for s in range(n_seq):
        rs = slice(s * SEQ_PAD, (s + 1) * SEQ_PAD)
        state = h0_ref[s].reshape(LANES, D_STATE)
        y_off.append(lax.dot_general(cm[rs], state, NT_DIMS, precision=HI,
                                     preferred_element_type=F32) * e_acs[rs])
        in_seq = (lane >= s * SEQ_PAD) & (lane < (s + 1) * SEQ_PAD)
        upd = jnp.dot(jnp.where(in_seq, xd_t, 0.0), bm, precision=HI, preferred_element_type=F32)
        new = state * cd_t[:, s * SEQ_PAD:s * SEQ_PAD + 1] + upd
        h1_ref[s] = new.reshape(2, SSM_HEADDIM, D_STATE)
    y_ref[...] = y + jnp.concatenate(y_off, axis=0)


def ssd_sample(xc, dt_exp, dtb_exp, a_exp, dsk_exp, h0, d_inner, t_valid, n_seq=16):
    rows = xc.shape[0]
    b = h0.shape[0]
    n_pairs = d_inner // LANES
    ppg = n_pairs // SSM_GROUPS
    ll = n_seq * SEQ_PAD
    first_b = d_inner // D_STATE
    return pl.pallas_call(
        functools.partial(_ssd_sample_body, n_seq=n_seq, t_valid=t_valid),
        grid=(b // n_seq, n_pairs),
        in_specs=[
            pl.BlockSpec((ll, LANES), lambda s, k: (s, k)),
            pl.BlockSpec((ll, D_STATE), lambda s, k: (s, first_b + k // ppg)),
            pl.BlockSpec((ll, D_STATE), lambda s, k: (s, first_b + SSM_GROUPS + k // ppg)),
            pl.BlockSpec((ll, LANES), lambda s, k: (s, k)),
            pl.BlockSpec((1, LANES), lambda s, k: (0, k)),
            pl.BlockSpec((1, LANES), lambda s, k: (0, k)),
            pl.BlockSpec((1, LANES), lambda s, k: (0, k)),
            pl.BlockSpec((n_seq, 2, SSM_HEADDIM, D_STATE), lambda s, k: (s, k, 0, 0)),
        ],
        out_specs=[
            pl.BlockSpec((ll, LANES), lambda s, k: (s, k)),
            pl.BlockSpec((n_seq, 2, SSM_HEADDIM, D_STATE), lambda s, k: (s, k, 0, 0)),
        ],
        out_shape=[
            jax.ShapeDtypeStruct((rows, d_inner), F32),
            jax.ShapeDtypeStruct(h0.shape, F32),
        ],
        compiler_params=_cparams(("arbitrary", "arbitrary"), 32),
        name="ssd_sample",
    )(xc, xc, xc, dt_exp, dtb_exp, a_exp, dsk_exp, h0)


def _gated_norm_body(y_ref, z_ref, g_ref, o_ref, *, d_inner):
    z = z_ref[...]
    yg = y_ref[...] * (z * _sigmoid(z))
    gw = d_inner // SSM_GROUPS
    for g in range(SSM_GROUPS):
        gs = slice(g * gw, (g + 1) * gw)
        v = yg[:, gs]
        r = lax.rsqrt(jnp.mean(v * v, axis=-1, keepdims=True) + EPS)
        o_ref[:, gs] = (v * r * g_ref[:, gs]).astype(BF16)


def gated_norm(y, z, norm_g, d_inner, tm=TM):
    t = y.shape[0]
    return pl.pallas_call(
        functools.partial(_gated_norm_body, d_inner=d_inner),
        grid=(t // tm,),
        in_specs=[
            pl.BlockSpec((tm, d_inner), lambda i: (i, 0)),
            pl.BlockSpec((tm, d_inner), lambda i: (i, 0)),
            _const_spec((1, d_inner)),
        ],
        out_specs=pl.BlockSpec((tm, d_inner), lambda i: (i, 0)),
        out_shape=jax.ShapeDtypeStruct((t, d_inner), BF16),
        compiler_params=_cparams(("arbitrary",), 32),
        name="gated_norm",
    )(y, z, norm_g.reshape(1, d_inner))


def _pack_cols(w, pieces, total):
    cols = [w[:, a:b] for a, b in pieces]
    used = sum(b - a for a, b in pieces)
    if total > used:
        cols.append(jnp.zeros((w.shape[0], total - used), w.dtype))
    return jnp.concatenate(cols, axis=1).astype(BF16)


def _pad_rows(a, rows, axis):
    pad = [(0, 0)] * a.ndim
    pad[axis] = (0, rows - a.shape[axis])
    return jnp.pad(a, pad)


def _dsa_layer(x_all, n_prompt, n_seq, t_s, mix_g, w_in, rel_bias, cache_k, cache_v, cache_kidx, page_table):
    qw, kvw, qiw = N_HEADS_A * HEAD_DIM, KV_HEADS * HEAD_DIM, IDX_HEADS * IDX_DIM
    o_q, o_k, o_v, o_qi = 0, qw, qw + kvw, qw + 2 * kvw
    o_ki = o_qi + qiw
    o_wi = o_ki + IDX_DIM
    o_qm = o_wi + IDX_HEADS
    col = {"q": 0, "k": qw, "qi": qw + kvw, "v": qw + kvw + qiw, "qm": qw + 2 * kvw + qiw}
    col["kw"] = col["qm"] + MEM_WIDTH
    width = col["kw"] + LANES
    w = _pack_cols(w_in, [(o_q, o_q + qw), (o_k, o_k + kvw), (o_qi, o_qi + qiw), (o_v, o_v + kvw),
                          (o_qm, o_qm + MEM_WIDTH), (o_ki, o_ki + IDX_DIM), (o_wi, o_wi + IDX_HEADS)], width)
    z = norm_matmul(x_all, mix_g, w, TM, width // 3)

    k_all = z[:, col["k"]:col["k"] + kvw]
    v_all = z[:, col["v"]:col["v"] + kvw]
    ki_all = z[:, col["kw"]:col["kw"] + IDX_DIM]

    ki_t = ki_all[:n_prompt].T.astype(BF16)
    zeros = jnp.zeros_like(ki_t)
    kta = jnp.concatenate([ki_t, zeros], axis=0)
    ktb = jnp.concatenate([zeros, ki_t], axis=0)
    mix_p = dsa_prompt(z, n_prompt, col, kta, ktb, k_all[:n_prompt].astype(BF16),
                       v_all[:n_prompt].astype(BF16), rel_bias)

    zs = z[n_prompt:].reshape(n_seq, t_s, width)
    zs8 = _pad_rows(zs, SEQ_PAD, 1)
    qi8 = zs8[:, :, col["qi"]:col["qi"] + qiw].reshape(n_seq, SEQ_PAD * IDX_HEADS, IDX_DIM)
    w8 = zs8[:, :, col["kw"] + IDX_DIM:col["kw"] + IDX_DIM + IDX_HEADS].reshape(n_seq, SEQ_PAD * IDX_HEADS, 1)
    q8 = zs8[:, :, :qw].reshape(n_seq, SEQ_PAD, N_HEADS_A, HEAD_DIM).transpose(0, 2, 1, 3)
    q8 = q8.reshape(n_seq, N_HEADS_A * SEQ_PAD, HEAD_DIM)
    kidx_new = _pad_rows(zs[:, :, col["kw"]:col["kw"] + IDX_DIM], PAGE_SIZE, 1)
    k_new = _pad_rows(zs[:, :, col["k"]:col["k"] + kvw], PAGE_SIZE, 1)
    v_new = _pad_rows(zs[:, :, col["v"]:col["v"] + kvw], PAGE_SIZE, 1)
    pool = cache_k.shape[0]
    o_s = dsa_sample(qi8, w8, q8, kidx_new, k_new, v_new, cache_kidx,
                     cache_k.reshape(pool, PAGE_SIZE, kvw), cache_v.reshape(pool, PAGE_SIZE, kvw),
                     page_table, rel_bias, t_s)
    mix_s = o_s.reshape(n_seq, N_HEADS_A, SEQ_PAD, HEAD_DIM)[:, :, :t_s].transpose(0, 2, 1, 3)
    mix_s = mix_s.reshape(n_seq * t_s, qw).astype(BF16)
    return z, col["qm"] // MEM_WIDTH, jnp.concatenate([mix_p, mix_s], axis=0), k_all, v_all, ki_all


def _ssd_layer(x_all, n_prompt, n_seq, t_s, mix_g, w_in, conv_w, conv_b, dt_bias, a_log, d_skip, norm_g,
               state_conv, state_ssm):
    n_heads = dt_bias.shape[0]
    d_inner = n_heads * SSM_HEADDIM
    cdim = d_inner + 2 * SSM_GROUPS * D_STATE
    o_x, o_dt = d_inner, d_inner + cdim
    o_qm = o_dt + n_heads
    c_qm = d_inner + cdim
    c_dt = c_qm + MEM_WIDTH
    width = _round_up(c_dt + n_heads, 2 * LANES)
    w = _pack_cols(w_in, [(0, d_inner), (o_x, o_x + cdim), (o_qm, o_qm + MEM_WIDTH), (o_dt, o_dt + n_heads)], width)
    z = norm_matmul(x_all, mix_g, w, TM, width // 2)
    xbc = z[:, d_inner:d_inner + cdim]
    dt_exp = jnp.repeat(z[:, c_dt:c_dt + n_heads], SSM_HEADDIM, axis=1)
    dtb_exp = jnp.repeat(dt_bias.astype(F32), SSM_HEADDIM).reshape(1, d_inner)
    a_exp = -jnp.exp(jnp.repeat(a_log.astype(F32), SSM_HEADDIM)).reshape(1, d_inner)
    dsk_exp = jnp.repeat(d_skip.astype(F32), SSM_HEADDIM).reshape(1, d_inner)

    xbc_p = xbc[:n_prompt][None]
    xc_p = conv_silu(xbc_p, 0, jnp.zeros((1, SUBLANES, cdim), F32), conv_w, conv_b, SSD_L, cdim)[0]
    y_p, hf_p = ssd_prompt(xc_p, dt_exp[:n_prompt], dtb_exp, a_exp, dsk_exp, d_inner)
    new_conv_p = xbc[n_prompt - (CONV_W - 1):n_prompt]

    xbc_s = xbc[n_prompt:].reshape(n_seq, t_s, cdim)
    xbc_s8 = _pad_rows(xbc_s, SEQ_PAD, 1)
    st8 = jnp.concatenate([jnp.zeros((n_seq, SUBLANES - (CONV_W - 1), cdim), F32), state_conv.astype(F32)], axis=1)
    xc_s = conv_silu(xbc_s8, 0, st8, conv_w, conv_b, SEQ_PAD, cdim).reshape(n_seq * SEQ_PAD, cdim)
    dt_s8 = _pad_rows(dt_exp[n_prompt:].reshape(n_seq, t_s, d_inner), SEQ_PAD, 1).reshape(n_seq * SEQ_PAD, d_inner)
    y_s8, hf_s = ssd_sample(xc_s, dt_s8, dtb_exp, a_exp, dsk_exp, state_ssm.astype(F32), d_inner, t_s)
    y_s = y_s8.reshape(n_seq, SEQ_PAD, d_inner)[:, :t_s].reshape(n_seq * t_s, d_inner)
    new_conv_s = jnp.concatenate([state_conv.astype(F32), xbc_s], axis=1)[:, -(CONV_W - 1):]

    mix = gated_norm(jnp.concatenate([y_p, y_s], axis=0), z, norm_g, d_inner)
    return z, c_qm // MEM_WIDTH, mix, hf_p, new_conv_p, hf_s, new_conv_s


def kernel(x_prompt, x_sample, mem_prompt, cache_k, cache_v, cache_kidx, page_table, state_ssm, state_conv,
           cache_mem_k, cache_mem_v, rel_bias, ffn1_g, ffn1_w_gu, ffn1_w_down, mix_g, mem_g, w_mem_kv,
           w_in_attn, w_in_ssd, conv_w, conv_b, dt_bias, a_log, d_skip, ssd_norm_g, w_out,
           ffn2_g, ffn2_w_gu, ffn2_w_down, final_g):
    bp, n_prompt, d = x_prompt.shape
    n_seq, t_s, _ = x_sample.shape
    assert bp == 1
    depth = ffn1_g.shape[0]
    n_mem = mem_prompt.shape[1]
    x_all = jnp.concatenate([x_prompt[0], x_sample.reshape(n_seq * t_s, d)], axis=0)
    outs = {k: [] for k in ("pk", "pv", "pki", "pssm", "pconv", "pmk", "pmv", "sk", "sv", "ski", "sssm", "sconv")}
    y_all = None
    for i in range(depth):
        j = i // 2
        x_all = ffn(x_all, ffn1_g[i], ffn1_w_gu[i], ffn1_w_down[i])
        mkv = norm_matmul(mem_prompt[0], mem_g[i], w_mem_kv[i].astype(BF16), n_mem, 2 * MEM_WIDTH)
        outs["pmk"].append(mkv[:, :MEM_WIDTH].reshape(1, n_mem, MEM_HEADS, MEM_HEAD_DIM))
        outs["pmv"].append(mkv[:, MEM_WIDTH:].reshape(1, n_mem, MEM_HEADS, MEM_HEAD_DIM))
        if i % 2 == 0:
            z, qm_blk, mix, k_all, v_all, ki_all = _dsa_layer(
                x_all, n_prompt, n_seq, t_s, mix_g[i], w_in_attn[j], rel_bias,
                cache_k[j], cache_v[j], cache_kidx[j], page_table)
            outs["pk"].append(k_all[:n_prompt].reshape(1, n_prompt, KV_HEADS, HEAD_DIM))
            outs["pv"].append(v_all[:n_prompt].reshape(1, n_prompt, KV_HEADS, HEAD_DIM))
            outs["pki"].append(ki_all[:n_prompt].reshape(1, n_prompt, IDX_DIM))
            outs["sk"].append(k_all[n_prompt:].reshape(n_seq, t_s, KV_HEADS, HEAD_DIM))
            outs["sv"].append(v_all[n_prompt:].reshape(n_seq, t_s, KV_HEADS, HEAD_DIM))
            outs["ski"].append(ki_all[n_prompt:].reshape(n_seq, t_s, IDX_DIM))
        else:
            z, qm_blk, mix, hf_p, conv_p, hf_s, conv_s = _ssd_layer(
                x_all, n_prompt, n_seq, t_s, mix_g[i], w_in_ssd[j], conv_w[j], conv_b[j], dt_bias[j],
                a_log[j], d_skip[j], ssd_norm_g[j], state_conv[j], state_ssm[j])
            outs["pssm"].append(hf_p[None])
            outs["pconv"].append(conv_p[None])
            outs["sssm"].append(hf_s)
            outs["sconv"].append(conv_s)
        mem_p = mem_attn_prompt(z, qm_blk, n_prompt, mkv)
        qm_s = z[n_prompt:, qm_blk * MEM_WIDTH:(qm_blk + 1) * MEM_WIDTH].reshape(n_seq, t_s, MEM_WIDTH)
        mem_s = mem_attn_sample(_pad_rows(qm_s, SEQ_PAD, 1),
                                cache_mem_k[i].reshape(n_seq, n_mem, MEM_WIDTH),
                                cache_mem_v[i].reshape(n_seq, n_mem, MEM_WIDTH))
        mem = jnp.concatenate([mem_p, mem_s[:, :t_s].reshape(n_seq * t_s, MEM_WIDTH)], axis=0)
        x_all = out_proj(x_all, mix, mem, w_out[i])
        if i == depth - 1:
            x_all, y_all = ffn(x_all, ffn2_g[i], ffn2_w_gu[i], ffn2_w_down[i], final_g=final_g)
        else:
            x_all = ffn(x_all, ffn2_g[i], ffn2_w_gu[i], ffn2_w_down[i])
    y_prompt = y_all[:n_prompt][None]
    y_sample = y_all[n_prompt:].reshape(n_seq, t_s, d)
    st = lambda k: jnp.stack(outs[k])
    return (y_prompt, y_sample, st("pk"), st("pv"), st("pki"), st("pssm"), st("pconv"), st("pmk"), st("pmv"),
            st("sk"), st("sv"), st("ski"), st("sssm"), st("sconv"))
```

```python
import functools
import math

import numpy as np
import jax
import jax.numpy as jnp
from jax import lax
from jax.experimental import pallas as pl
from jax.experimental.pallas import tpu as pltpu

F32, BF16, I32 = jnp.float32, jnp.bfloat16, jnp.int32
HI = lax.Precision.HIGHEST
NT_DIMS = (((1,), (1,)), ((), ()))

HEAD_DIM = 128
KV_HEADS = 4
Q_PER_KV = 3
N_HEADS_A = KV_HEADS * Q_PER_KV
IDX_HEADS = 16
IDX_DIM = 64
TOPK_MAX = 256
REL_BUCKETS = 32
REL_MAX_EXACT = 16
REL_MAX_DIST = 128
MEM_HEADS = 4
MEM_HEAD_DIM = 128
MEM_WIDTH = MEM_HEADS * MEM_HEAD_DIM
SSM_HEADDIM = 64
SSM_GROUPS = 4
D_STATE = 128
CONV_W = 4
PAGE_SIZE = 128
EPS = 1e-6

LANES = 128
SUBLANES = 8
NEG = -1e30
INT_MIN = -2 ** 31
IDX_SCALE = IDX_DIM ** -0.5 * IDX_HEADS ** -0.5
ATT_SCALE = HEAD_DIM ** -0.5
MEM_SCALE = MEM_HEAD_DIM ** -0.5

QB = 128
KC = 512
TM = 512
TF = 512
SSD_L = 128
SEQ_PAD = 8


def _cparams(sem, vmem_mb):
    return pltpu.CompilerParams(dimension_semantics=sem, vmem_limit_bytes=vmem_mb * 2 ** 20)


def _round_up(n, m):
    return (n + m - 1) // m * m


def _const_spec(shape):
    nd = len(shape)
    return pl.BlockSpec(shape, lambda *_: (0,) * nd)


def _resident_spec(shape):
    nd = len(shape)
    return pl.BlockSpec(shape, lambda *_: (0,) * nd, pipeline_mode=pl.Buffered(1))


def _rms(x, g):
    return x * lax.rsqrt(jnp.mean(x * x, axis=-1, keepdims=True) + EPS) * g


def _sigmoid(x):
    return 1.0 / (1.0 + jnp.exp(-x))


def _softplus(x):
    return jnp.maximum(x, 0.0) + jnp.log1p(jnp.exp(-jnp.abs(x)))


def _ffn_body(x_ref, g_ref, wgu_ref, wd_ref, *rest, tf, nj, with_final):
    if with_final:
        fg_ref, o_ref, y_ref, xn_ref, acc_ref = rest
    else:
        o_ref, xn_ref, acc_ref = rest
    j = pl.program_id(1)

    @pl.when(j == 0)
    def _():
        xn_ref[...] = _rms(x_ref[...], g_ref[...]).astype(BF16)
        acc_ref[...] = jnp.zeros_like(acc_ref)

    h = jnp.dot(xn_ref[...], wgu_ref[...], preferred_element_type=F32)
    gate, up = h[:, :tf], h[:, tf:]
    a = gate * _sigmoid(gate) * up
    acc_ref[...] += jnp.dot(a.astype(BF16), wd_ref[...], preferred_element_type=F32)

    @pl.when(j == nj - 1)
    def _():
        o = x_ref[...] + 0.5 * acc_ref[...]
        o_ref[...] = o
        if with_final:
            y_ref[...] = _rms(o, fg_ref[...])


def _prep_ffn(w_gu, w_d, tf):
    d, two_ff = w_gu.shape
    ff = two_ff // 2
    ffp = _round_up(ff, tf)
    nj = ffp // tf
    pad = ((0, 0), (0, ffp - ff))
    wg = jnp.pad(w_gu[:, :ff].astype(BF16), pad).reshape(d, nj, tf)
    wu = jnp.pad(w_gu[:, ff:].astype(BF16), pad).reshape(d, nj, tf)
    wgu = jnp.concatenate([wg, wu], axis=2).reshape(d, nj * 2 * tf)
    wd = jnp.pad(w_d.astype(BF16), ((0, ffp - ff), (0, 0)))
    return wgu, wd, nj


def ffn(x, g, w_gu, w_d, final_g=None, tm=TM, tf=TF):
    t, d = x.shape
    wgu, wd, nj = _prep_ffn(w_gu, w_d, tf)
    with_final = final_g is not None
    in_specs = [
        pl.BlockSpec((tm, d), lambda i, j: (i, 0)),
        _const_spec((1, d)),
        pl.BlockSpec((d, 2 * tf), lambda i, j: (0, j)),
        pl.BlockSpec((tf, d), lambda i, j: (j, 0)),
    ]
    args = [x, g.reshape(1, d), wgu, wd]
    out_shape = [jax.ShapeDtypeStruct((t, d), F32)]
    out_specs = [pl.BlockSpec((tm, d), lambda i, j: (i, 0))]
    if with_final:
        in_specs.append(_const_spec((1, d)))
        args.append(final_g.reshape(1, d))
        out_shape.append(jax.ShapeDtypeStruct((t, d), F32))
        out_specs.append(pl.BlockSpec((tm, d), lambda i, j: (i, 0)))
    res = pl.pallas_call(
        functools.partial(_ffn_body, tf=tf, nj=nj, with_final=with_final),
        grid=(t // tm, nj),
        in_specs=in_specs,
        out_specs=out_specs,
        out_shape=out_shape,
        scratch_shapes=[pltpu.VMEM((tm, d), BF16), pltpu.VMEM((tm, d), F32)],
        compiler_params=_cparams(("arbitrary", "arbitrary"), 56),
        name="ffn",
    )(*args)
    return res if with_final else res[0]


def _norm_matmul_body(x_ref, g_ref, w_ref, o_ref, xn_ref):
    @pl.when(pl.program_id(1) == 0)
    def _():
        xn_ref[...] = _rms(x_ref[...], g_ref[...]).astype(BF16)

    o_ref[...] = jnp.dot(xn_ref[...], w_ref[...], preferred_element_type=F32)


def norm_matmul(x, g, w_bf16, tm, tn):
    t, d = x.shape
    n = w_bf16.shape[1]
    return pl.pallas_call(
        _norm_matmul_body,
        grid=(t // tm, n // tn),
        in_specs=[
            pl.BlockSpec((tm, d), lambda i, j: (i, 0)),
            _const_spec((1, d)),
            pl.BlockSpec((d, tn), lambda i, j: (0, j)),
        ],
        out_specs=pl.BlockSpec((tm, tn), lambda i, j: (i, j)),
        out_shape=jax.ShapeDtypeStruct((t, n), F32),
        scratch_shapes=[pltpu.VMEM((tm, d), BF16)],
        compiler_params=_cparams(("arbitrary", "arbitrary"), 48),
        name="norm_matmul",
    )(x, g.reshape(1, d), w_bf16)


def _out_proj_body(x_ref, mix_ref, mem_ref, w1_ref, w2_ref, o_ref):
    o_ref[...] = (x_ref[...]
                  + jnp.dot(mix_ref[...], w1_ref[...], preferred_element_type=F32)
                  + jnp.dot(mem_ref[...], w2_ref[...], preferred_element_type=F32))


def out_proj(x, mix, mem, w_out, tm=TM):
    t, d = x.shape
    dm, dw = mix.shape[1], mem.shape[1]
    w1 = w_out[:dm].astype(BF16)
    w2 = w_out[dm:].astype(BF16)
    return pl.pallas_call(
        _out_proj_body,
        grid=(t // tm,),
        in_specs=[
            pl.BlockSpec((tm, d), lambda i: (i, 0)),
            pl.BlockSpec((tm, dm), lambda i: (i, 0)),
            pl.BlockSpec((tm, dw), lambda i: (i, 0)),
            _const_spec((dm, d)),
            _const_spec((dw, d)),
        ],
        out_specs=pl.BlockSpec((tm, d), lambda i: (i, 0)),
        out_shape=jax.ShapeDtypeStruct((t, d), F32),
        compiler_params=_cparams(("arbitrary",), 48),
        name="out_proj",
    )(x, mix, mem, w1, w2)


def _softmax_rows(s):
    m = jnp.max(s, axis=-1, keepdims=True)
    p = jnp.exp(s - m)
    return p, jnp.sum(p, axis=-1, keepdims=True)


def _mem_attn_prompt_body(q_ref, mk_ref, mv_ref, o_ref):
    for h in range(MEM_HEADS):
        sl = slice(h * MEM_HEAD_DIM, (h + 1) * MEM_HEAD_DIM)
        q = (q_ref[:, sl] * MEM_SCALE).astype(BF16)
        s = lax.dot_general(q, mk_ref[:, sl].astype(BF16), NT_DIMS, preferred_element_type=F32)
        p, l = _softmax_rows(s)
        o = jnp.dot(p.astype(BF16), mv_ref[:, sl].astype(BF16), preferred_element_type=F32)
        o_ref[:, sl] = (o / l).astype(BF16)


def mem_attn_prompt(z, qm_col_block, n_rows, mkv, tm=TM):
    m = mkv.shape[0]
    return pl.pallas_call(
        _mem_attn_prompt_body,
        grid=(n_rows // tm,),
        in_specs=[
            pl.BlockSpec((tm, MEM_WIDTH), lambda i: (i, qm_col_block)),
            pl.BlockSpec((m, MEM_WIDTH), lambda i: (0, 0)),
            pl.BlockSpec((m, MEM_WIDTH), lambda i: (0, 1)),
        ],
        out_specs=pl.BlockSpec((tm, MEM_WIDTH), lambda i: (i, 0)),
        out_shape=jax.ShapeDtypeStruct((n_rows, MEM_WIDTH), BF16),
        compiler_params=_cparams(("arbitrary",), 32),
        name="mem_attn_prompt",
    )(z, mkv, mkv)


def _mem_attn_sample_body(q_ref, k_ref, v_ref, o_ref, *, bb):
    for b in range(bb):
        for h in range(MEM_HEADS):
            sl = slice(h * MEM_HEAD_DIM, (h + 1) * MEM_HEAD_DIM)
            q = (q_ref[b, :, sl] * MEM_SCALE).astype(BF16)
            s = lax.dot_general(q, k_ref[b, :, h, :].astype(BF16), NT_DIMS, preferred_element_type=F32)
            p, l = _softmax_rows(s)
            o = jnp.dot(p.astype(BF16), v_ref[b, :, h, :].astype(BF16), preferred_element_type=F32)
            o_ref[b, :, sl] = (o / l).astype(BF16)


def mem_attn_sample(qm8, mem_k, mem_v, layer, bb=8):
    _, b, m, nh, hd = mem_k.shape
    cache_spec = pl.BlockSpec((None, bb, m, nh, hd), lambda i: (layer, i, 0, 0, 0))
    return pl.pallas_call(
        functools.partial(_mem_attn_sample_body, bb=bb),
        grid=(b // bb,),
        in_specs=[pl.BlockSpec((bb, SEQ_PAD, MEM_WIDTH), lambda i: (i, 0, 0)), cache_spec, cache_spec],
        out_specs=pl.BlockSpec((bb, SEQ_PAD, MEM_WIDTH), lambda i: (i, 0, 0)),
        out_shape=jax.ShapeDtypeStruct((b, SEQ_PAD, MEM_WIDTH), BF16),
        compiler_params=_cparams(("arbitrary",), 40),
        name="mem_attn_sample",
    )(qm8, mem_k, mem_v)


def _sortable_key(x):
    b = pltpu.bitcast(x, I32)
    return jnp.where(b < 0, (b ^ 0x7FFFFFFF) + 1, b)


def _t5_bucket_np(dist):
    n = np.maximum(dist, 0)
    nf = np.maximum(n, 1).astype(np.float64)
    large = REL_MAX_EXACT + (np.log(nf / REL_MAX_EXACT) / math.log(REL_MAX_DIST / REL_MAX_EXACT)
                             * (REL_BUCKETS - REL_MAX_EXACT)).astype(np.int32)
    large = np.minimum(large, REL_BUCKETS - 1)
    return np.where(n < REL_MAX_EXACT, n, large)


FAR_DIST = int(np.min(np.nonzero(_t5_bucket_np(np.arange(4 * REL_MAX_DIST)) == REL_BUCKETS - 1)[0]))
assert np.all(_t5_bucket_np(np.arange(FAR_DIST, 1 << 16)) == REL_BUCKETS - 1) and FAR_DIST <= QB


def _toeplitz_bias(rel_bias, n, m, k0):
    p = n + m
    d = k0 + (n - 1) - np.arange(p)
    b = _t5_bucket_np(d)
    keep = (d >= 0) & (b != REL_BUCKETS - 1)
    u = jnp.where(keep[:, None], rel_bias[b] - rel_bias[REL_BUCKETS - 1], 0.0).T
    skew = jnp.tile(u, (1, n))[:, :n * (p - 1)].reshape(-1, n, p - 1)
    return skew[:, :, n - 1:n - 1 + m].astype(F32)


def _threshold_search(count_ge, count_tie_lt, shape, topk, n_idx_bits, all_idx):
    zero = jnp.zeros(shape, I32)
    t0 = jnp.where(count_ge(zero) >= topk, zero, jnp.full(shape, INT_MIN, I32))

    def bit_body(b, t):
        cand = t | lax.shift_left(jnp.int32(1), 30 - b)
        return jnp.where(count_ge(cand) >= topk, cand, t)

    t = lax.fori_loop(0, 31, bit_body, t0)
    has_k = t > INT_MIN
    n_ge = count_ge(t)
    need = topk - count_ge(t + 1)
    excess = jnp.max(jnp.where(has_k, n_ge, 0.0)) > topk

    def tie_search(_):
        def jbit(b, j):
            cand = j | lax.shift_left(jnp.int32(1), n_idx_bits - 1 - b)
            return jnp.where(count_tie_lt(t, cand) < need, cand, j)
        return lax.fori_loop(0, n_idx_bits, jbit, jnp.zeros(shape, I32))

    j = lax.cond(excess, tie_search, lambda _: jnp.full(shape, all_idx, I32), 0)
    j = jnp.where(has_k, j, -1)
    return t, j


def _dsa_prompt_body(q_ref, qi_ref, kw_ref, kiw_ref, k_ref, v_ref, bias_ref, o_ref,
                     rhs_ref, wt_ref, key_ref, qe_ref, m_ref, acc_ref, *, seq, topk):
    i = pl.program_id(0)
    q0 = i * QB

    qi_t = qi_ref[...].T
    pad = jnp.zeros((LANES - IDX_DIM, QB), F32)
    for h in range(IDX_HEADS):
        blk = jnp.concatenate([qi_t[h * IDX_DIM:(h + 1) * IDX_DIM], pad], axis=0)
        rhs_ref[:, h * QB:(h + 1) * QB] = blk.astype(BF16)
    wt_ref[...] = kw_ref[...].T * IDX_SCALE

    n_chunks = (jnp.maximum(i + 1, 2) * QB + KC - 1) // KC

    def score_chunk(c, carry):
        c0 = pl.multiple_of(c * KC, KC)
        x = jnp.dot(kiw_ref[pl.ds(c0, KC), :], rhs_ref[...], preferred_element_type=F32)
        acc = jnp.zeros((KC, QB), F32)
        for h in range(IDX_HEADS):
            acc = acc + jnp.maximum(x[:, h * QB:(h + 1) * QB], 0.0) * wt_ref[IDX_DIM + h:IDX_DIM + h + 1, :]
        kpos = c0 + lax.broadcasted_iota(I32, (KC, QB), 0)
        qpos = q0 + lax.broadcasted_iota(I32, (KC, QB), 1)
        key_ref[pl.ds(c0, KC), :] = jnp.where(kpos <= qpos, _sortable_key(acc), INT_MIN)
        return carry

    lax.fori_loop(0, n_chunks, score_chunk, 0)

    def column_count(hit_of_chunk):
        acc_rows = 8 * SUBLANES

        def body(c, a):
            c0 = pl.multiple_of(c * KC, KC)
            hit = jnp.where(hit_of_chunk(key_ref[pl.ds(c0, KC), :], c0), 1.0, 0.0)
            return a + hit.reshape(KC // acc_rows, acc_rows, QB).sum(axis=0)

        a = lax.fori_loop(0, n_chunks, body, jnp.zeros((acc_rows, QB), F32))
        return jnp.sum(a, axis=0, keepdims=True)

    def count_ge(cand):
        return column_count(lambda kk, c0: kk >= cand)

    def count_tie_lt(t, jc):
        row = lax.broadcasted_iota(I32, (KC, QB), 0)
        return column_count(lambda kk, c0: (kk == t) & (c0 + row < jc))

    t, j = _threshold_search(count_ge, count_tie_lt, (1, QB), topk, (seq - 1).bit_length(), seq)

    eye = jnp.where(lax.broadcasted_iota(I32, (QB, QB), 0) == lax.broadcasted_iota(I32, (QB, QB), 1),
                    1.0, 0.0).astype(BF16)
    for g in range(KV_HEADS):
        for r in range(Q_PER_KV):
            hq = g * Q_PER_KV + r
            rs = slice(r * QB, (r + 1) * QB)
            qe_ref[g, rs, 0:HEAD_DIM] = (q_ref[:, hq * HEAD_DIM:(hq + 1) * HEAD_DIM] * ATT_SCALE).astype(BF16)
            qe_ref[g, rs, HEAD_DIM:2 * HEAD_DIM] = eye
    m_ref[...] = jnp.full(m_ref.shape, 3 * NEG, F32)
    acc_ref[...] = jnp.zeros(acc_ref.shape, F32)

    def attn_chunk(c0, width, far_end, bias_of_group):
        kk = key_ref[pl.ds(c0, width), :]
        kpos = c0 + lax.broadcasted_iota(I32, (width, QB), 0)
        sel = (kk > t) | ((kk == t) & (kpos <= j))
        if far_end is not None:
            sel = sel & (kpos < far_end)
        madd_t = jnp.where(sel, 0.0, NEG).astype(BF16)
        ones = jnp.ones((width, HEAD_DIM), BF16)
        for g in range(KV_HEADS):
            gs = slice(g * HEAD_DIM, (g + 1) * HEAD_DIM)
            k_ext = jnp.concatenate([k_ref[pl.ds(c0, width), gs], madd_t], axis=1)
            s = lax.dot_general(qe_ref[g], k_ext, NT_DIMS, preferred_element_type=F32)
            if bias_of_group is not None:
                s = s + bias_of_group(g)
            m_old = m_ref[g]
            m_new = jnp.maximum(m_old, jnp.max(s, axis=1, keepdims=True))
            alpha = jnp.exp(m_old - m_new)
            p = jnp.exp((s - m_new[:, 0:1]).astype(BF16))
            v_ext = jnp.concatenate([v_ref[pl.ds(c0, width), gs], ones], axis=1)
            acc_ref[g] = (jnp.concatenate([alpha, alpha], axis=1) * acc_ref[g]
                          + jnp.dot(p, v_ext, preferred_element_type=F32))
            m_ref[g] = m_new

    far_end = jnp.maximum(i - 1, 0) * QB

    def far_body(c, carry):
        attn_chunk(pl.multiple_of(c * KC, KC), KC, far_end, None)
        return carry

    lax.fori_loop(0, (far_end + KC - 1) // KC, far_body, 0)
    variant = jnp.minimum(i, 1)
    attn_chunk(pl.multiple_of(far_end, QB), 2 * QB, None, lambda g: bias_ref[variant, g])

    for g in range(KV_HEADS):
        o = acc_ref[g, :, 0:HEAD_DIM] / acc_ref[g, :, HEAD_DIM:2 * HEAD_DIM]
        for r in range(Q_PER_KV):
            hq = g * Q_PER_KV + r
            o_ref[:, hq * HEAD_DIM:(hq + 1) * HEAD_DIM] = o[r * QB:(r + 1) * QB].astype(BF16)


def _prompt_bias(rel_bias):
    out = [_toeplitz_bias(rel_bias, QB, 2 * QB, k0).reshape(KV_HEADS, Q_PER_KV * QB, 2 * QB) for k0 in (0, QB)]
    return jnp.stack(out)


def dsa_prompt(z, seq, col, kiw_bf, k_bf, v_bf, rel_bias):
    topk = min(TOPK_MAX, seq // 4)
    assert seq % KC == 0 and seq >= 2 * QB
    qw = N_HEADS_A * HEAD_DIM
    qiw = IDX_HEADS * IDX_DIM
    bias = _prompt_bias(rel_bias)
    rows = Q_PER_KV * QB
    return pl.pallas_call(
        functools.partial(_dsa_prompt_body, seq=seq, topk=topk),
        grid=(seq // QB,),
        in_specs=[
            pl.BlockSpec((QB, qw), lambda i: (i, col["q"] // qw)),
            pl.BlockSpec((QB, qiw), lambda i: (i, col["qi"] // qiw)),
            pl.BlockSpec((QB, LANES), lambda i: (i, col["kw"] // LANES)),
            _resident_spec((seq, LANES)),
            _resident_spec((seq, KV_HEADS * HEAD_DIM)),
            _resident_spec((seq, KV_HEADS * HEAD_DIM)),
            _resident_spec(bias.shape),
        ],
        out_specs=pl.BlockSpec((QB, qw), lambda i: (i, 0)),
        out_shape=jax.ShapeDtypeStruct((seq, qw), BF16),
        scratch_shapes=[
            pltpu.VMEM((LANES, IDX_HEADS * QB), BF16),
            pltpu.VMEM((LANES, QB), F32),
            pltpu.VMEM((seq, QB), I32),
            pltpu.VMEM((KV_HEADS, rows, 2 * HEAD_DIM), BF16),
            pltpu.VMEM((KV_HEADS, rows, LANES), F32),
            pltpu.VMEM((KV_HEADS, rows, 2 * HEAD_DIM), F32),
        ],
        compiler_params=_cparams(("arbitrary",), 56),
        name="dsa_prompt",
    )(z, z, z, kiw_bf, k_bf, v_bf, bias)


def _dsa_sample_body(pt_ref, qi_ref, w_ref, q_ref, bias_ref, *refs, n_pages, t_valid, topk):
    del pt_ref
    np1 = n_pages + 1
    kidx_refs, k_refs, v_refs = refs[0:np1], refs[np1:2 * np1], refs[2 * np1:3 * np1]
    o_ref, key_ref, s_ref = refs[3 * np1:]
    past = n_pages * PAGE_SIZE
    width = np1 * PAGE_SIZE
    rows = KV_HEADS * Q_PER_KV * SEQ_PAD

    qi = qi_ref[...].astype(BF16)
    w = w_ref[...] * IDX_SCALE
    for p in range(np1):
        x = lax.dot_general(qi, kidx_refs[p][...].astype(BF16), NT_DIMS, preferred_element_type=F32)
        sc = (jnp.maximum(x, 0.0) * w).reshape(SEQ_PAD, IDX_HEADS, PAGE_SIZE).sum(axis=1)
        kpos = p * PAGE_SIZE + lax.broadcasted_iota(I32, (SEQ_PAD, PAGE_SIZE), 1)
        qpos = past + lax.broadcasted_iota(I32, (SEQ_PAD, PAGE_SIZE), 0)
        ok = (kpos <= qpos) & (kpos < past + t_valid)
        key_ref[:, p * PAGE_SIZE:(p + 1) * PAGE_SIZE] = jnp.where(ok, _sortable_key(sc), INT_MIN)

    keys = key_ref[...]
    kpos = lax.broadcasted_iota(I32, (SEQ_PAD, width), 1)

    def count_ge(cand):
        return jnp.sum(jnp.where(keys >= cand, 1.0, 0.0), axis=1, keepdims=True)

    def count_tie_lt(t, jc):
        return jnp.sum(jnp.where((keys == t) & (kpos < jc), 1.0, 0.0), axis=1, keepdims=True)

    t, j = _threshold_search(count_ge, count_tie_lt, (SEQ_PAD, 1), topk, (width - 1).bit_length(), width)
    sel = (keys > t) | ((keys == t) & (kpos <= j))
    madd = jnp.where(sel, 0.0, NEG)
    madd = jnp.concatenate([madd] * (KV_HEADS * Q_PER_KV), axis=0)

    def kv_tile(page_refs, p, g):
        if p < n_pages:
            return page_refs[p][:, g, :].astype(BF16)
        return page_refs[p][:, g * HEAD_DIM:(g + 1) * HEAD_DIM].astype(BF16)

    grp_rows = Q_PER_KV * SEQ_PAD
    for g in range(KV_HEADS):
        rs = slice(g * grp_rows, (g + 1) * grp_rows)
        qg = (q_ref[rs, :] * ATT_SCALE).astype(BF16)
        for p in range(np1):
            s_ref[rs, p * PAGE_SIZE:(p + 1) * PAGE_SIZE] = lax.dot_general(
                qg, kv_tile(k_refs, p, g), NT_DIMS, preferred_element_type=F32)
    pr, l = _softmax_rows(s_ref[...] + bias_ref[...] + madd)
    for g in range(KV_HEADS):
        rs = slice(g * grp_rows, (g + 1) * grp_rows)
        acc = jnp.zeros((grp_rows, HEAD_DIM), F32)
        for p in range(np1):
            acc = acc + jnp.dot(pr[rs, p * PAGE_SIZE:(p + 1) * PAGE_SIZE].astype(BF16),
                                kv_tile(v_refs, p, g), preferred_element_type=F32)
        o_ref[rs, :] = acc / l[rs]


def _sample_bias(rel_bias, past, width):
    near = width - (past - PAGE_SIZE)
    b = _toeplitz_bias(rel_bias, SEQ_PAD, near, PAGE_SIZE).reshape(N_HEADS_A * SEQ_PAD, near)
    return jnp.pad(b, ((0, 0), (width - near, 0)))


def dsa_sample(qi8, w8, q8, kidx_new, k_new, v_new, cache_kidx, cache_k, cache_v, layer, page_table,
               rel_bias, t_valid):
    b, n_pages = page_table.shape
    np1 = n_pages + 1
    past = n_pages * PAGE_SIZE
    width = np1 * PAGE_SIZE
    topk = min(TOPK_MAX, (past + t_valid) // 4)
    bias = _sample_bias(rel_bias, past, width)
    rows = N_HEADS_A * SEQ_PAD
    kvw = KV_HEADS * HEAD_DIM

    def page_spec(tail, p):
        zeros = (0,) * len(tail)
        return pl.BlockSpec((None, None) + tail, lambda i, pt: (layer, pt[i, p]) + zeros)

    def new_spec(last):
        return pl.BlockSpec((None, PAGE_SIZE, last), lambda i, pt: (i, 0, 0))

    in_specs = [
        pl.BlockSpec((None, SEQ_PAD * IDX_HEADS, IDX_DIM), lambda i, pt: (i, 0, 0)),
        pl.BlockSpec((None, SEQ_PAD * IDX_HEADS, 1), lambda i, pt: (i, 0, 0)),
        pl.BlockSpec((None, rows, HEAD_DIM), lambda i, pt: (i, 0, 0)),
        pl.BlockSpec((rows, width), lambda i, pt: (0, 0)),
    ]
    args = [qi8, w8, q8, bias]
    kv_tail = (PAGE_SIZE, KV_HEADS, HEAD_DIM)
    for arr_cache, arr_new, tail, last in ((cache_kidx, kidx_new, (PAGE_SIZE, IDX_DIM), IDX_DIM),
                                           (cache_k, k_new, kv_tail, kvw), (cache_v, v_new, kv_tail, kvw)):
        in_specs += [page_spec(tail, p) for p in range(n_pages)] + [new_spec(last)]
        args += [arr_cache] * n_pages + [arr_new]
    return pl.pallas_call(
        functools.partial(_dsa_sample_body, n_pages=n_pages, t_valid=t_valid, topk=topk),
        grid_spec=pltpu.PrefetchScalarGridSpec(
            num_scalar_prefetch=1,
            grid=(b,),
            in_specs=in_specs,
            out_specs=pl.BlockSpec((None, rows, HEAD_DIM), lambda i, pt: (i, 0, 0)),
            scratch_shapes=[pltpu.VMEM((SEQ_PAD, width), I32), pltpu.VMEM((rows, width), F32)],
        ),
        out_shape=jax.ShapeDtypeStruct((b, rows, HEAD_DIM), F32),
        compiler_params=_cparams(("arbitrary",), 48),
        name="dsa_sample",
    )(page_table, *args)


def _conv_body(x_ref, st_ref, w_ref, b_ref, o_ref, tail_ref, *, rows):
    @pl.when(pl.program_id(1) == 0)
    def _():
        tail_ref[...] = st_ref[...]

    x = x_ref[...]
    xc = jnp.concatenate([tail_ref[...], x], axis=0)
    out = b_ref[...] + x * w_ref[CONV_W - 1:CONV_W, :]
    for k in range(1, CONV_W):
        shifted = pltpu.roll(xc, k, 0)[SUBLANES:SUBLANES + rows]
        out = out + shifted * w_ref[CONV_W - 1 - k:CONV_W - k, :]
    o_ref[...] = out * _sigmoid(out)
    tail_ref[...] = x[rows - SUBLANES:rows]


def conv_silu(z3, col_block, state8, conv_w, conv_b, rows, cdim):
    b, t, _ = z3.shape
    return pl.pallas_call(
        functools.partial(_conv_body, rows=rows),
        grid=(b, t // rows),
        in_specs=[
            pl.BlockSpec((None, rows, cdim), lambda i, c: (i, c, col_block)),
            pl.BlockSpec((None, SUBLANES, cdim), lambda i, c: (i, 0, 0)),
            _const_spec((CONV_W, cdim)),
            _const_spec((1, cdim)),
        ],
        out_specs=pl.BlockSpec((None, rows, cdim), lambda i, c: (i, c, 0)),
        out_shape=jax.ShapeDtypeStruct((b, t, cdim), F32),
        scratch_shapes=[pltpu.VMEM((SUBLANES, cdim), F32)],
        compiler_params=_cparams(("arbitrary", "arbitrary"), 32),
        name="conv_silu",
    )(z3, state8, conv_w, conv_b.reshape(1, cdim))


def _ssd_pair(xs, dt_raw, dtb, a, dskip, bm, cm, cb, lmask, valid):
    dt = _softplus(dt_raw + dtb)
    if valid is not None:
        dt = jnp.where(valid, dt, 0.0)
    xdt = xs * dt
    acs = jnp.dot(jnp.where(lmask, 1.0, 0.0), dt * a, precision=HI, preferred_element_type=F32)
    acs_t = acs.T
    half = SSM_HEADDIM
    yd = []
    for lane0 in (0, half):
        seg = acs[:, lane0:lane0 + 1] - acs_t[lane0:lane0 + 1, :]
        lm = jnp.where(lmask, jnp.exp(jnp.where(lmask, seg, 0.0)), 0.0)
        yd.append(jnp.dot(cb * lm, xdt, precision=HI, preferred_element_type=F32))
    lane = lax.broadcasted_iota(I32, xs.shape, 1)
    y = jnp.where(lane < half, yd[0], yd[1]) + dskip * xs
    return xdt, acs, y


def _ssd_prompt_body(xs_ref, bm_ref, cm_ref, dt_ref, dtb_ref, a_ref, dsk_ref, y_ref, hf_ref, st_ref,
                     *, n_pairs, n_chunks):
    c = pl.program_id(0)

    @pl.when(c == 0)
    def _():
        st_ref[...] = jnp.zeros_like(st_ref)

    ll = SSD_L
    li = lax.broadcasted_iota(I32, (ll, ll), 0)
    si = lax.broadcasted_iota(I32, (ll, ll), 1)
    lmask = si <= li
    pairs_per_group = n_pairs // SSM_GROUPS
    for g in range(SSM_GROUPS):
        gs = slice(g * D_STATE, (g + 1) * D_STATE)
        bm = bm_ref[:, gs]
        cm = cm_ref[:, gs]
        cb = lax.dot_general(cm, bm, NT_DIMS, precision=HI, preferred_element_type=F32)
        for kk in range(pairs_per_group):
            k = g * pairs_per_group + kk
            ks = slice(k * LANES, (k + 1) * LANES)
            xdt, acs, y = _ssd_pair(xs_ref[:, ks], dt_ref[:, ks], dtb_ref[:, ks], a_ref[:, ks],
                                    dsk_ref[:, ks], bm, cm, cb, lmask, None)
            acs_last = acs[ll - 1:ll, :]
            state = st_ref[k]
            y_off = lax.dot_general(cm, state, NT_DIMS, precision=HI, preferred_element_type=F32)
            y_ref[:, ks] = y + y_off * jnp.exp(acs)
            xd_t = (xdt * jnp.exp(acs_last - acs)).T
            upd = jnp.dot(xd_t, bm, precision=HI, preferred_element_type=F32)
            cd = jnp.exp(jnp.broadcast_to(acs_last, (ll, LANES))).T[:, 0:1]
            st_ref[k] = state * cd + upd

    @pl.when(c == n_chunks - 1)
    def _():
        hf_ref[...] = st_ref[...]


def ssd_prompt(xc, dt_exp, dtb_exp, a_exp, dsk_exp, d_inner):
    t = xc.shape[0]
    n_pairs = d_inner // LANES
    gn = SSM_GROUPS * D_STATE
    n_chunks = t // SSD_L
    y, hf = pl.pallas_call(
        functools.partial(_ssd_prompt_body, n_pairs=n_pairs, n_chunks=n_chunks),
        grid=(n_chunks,),
        in_specs=[
            pl.BlockSpec((SSD_L, d_inner), lambda c: (c, 0)),
            pl.BlockSpec((SSD_L, gn), lambda c: (c, d_inner // gn)),
            pl.BlockSpec((SSD_L, gn), lambda c: (c, d_inner // gn + 1)),
            pl.BlockSpec((SSD_L, d_inner), lambda c: (c, 0)),
            _const_spec((1, d_inner)),
            _const_spec((1, d_inner)),
            _const_spec((1, d_inner)),
        ],
        out_specs=[
            pl.BlockSpec((SSD_L, d_inner), lambda c: (c, 0)),
            _const_spec((n_pairs, LANES, D_STATE)),
        ],
        out_shape=[
            jax.ShapeDtypeStruct((t, d_inner), F32),
            jax.ShapeDtypeStruct((n_pairs, LANES, D_STATE), F32),
        ],
        scratch_shapes=[pltpu.VMEM((n_pairs, LANES, D_STATE), F32)],
        compiler_params=_cparams(("arbitrary",), 32),
        name="ssd_prompt",
    )(xc, xc, xc, dt_exp, dtb_exp, a_exp, dsk_exp)
    return y, hf.reshape(2 * n_pairs, SSM_HEADDIM, D_STATE)


def _ssd_sample_body(xs_ref, bm_ref, cm_ref, dt_ref, dtb_ref, a_ref, dsk_ref, h0_ref, y_ref, h1_ref,
                     *, n_seq, t_valid):
    ll = n_seq * SEQ_PAD
    li = lax.broadcasted_iota(I32, (ll, ll), 0)
    si = lax.broadcasted_iota(I32, (ll, ll), 1)
    same = lax.shift_right_logical(li, 3) == lax.shift_right_logical(si, 3)
    lmask = same & (si <= li)
    last = same & ((si & (SEQ_PAD - 1)) == SEQ_PAD - 1)
    rowi = lax.broadcasted_iota(I32, (ll, LANES), 0)
    valid = (rowi & (SEQ_PAD - 1)) < t_valid
    bm = bm_ref[...]
    cm = cm_ref[...]
    cb = lax.dot_general(cm, bm, NT_DIMS, precision=HI, preferred_element_type=F32)
    xdt, acs, y = _ssd_pair(xs_ref[...], dt_ref[...], dtb_ref[...], a_ref[...], dsk_ref[...],
                            bm, cm, cb, lmask, valid)
    acs_last = jnp.dot(jnp.where(last, 1.0, 0.0), acs, precision=HI, preferred_element_type=F32)
    e_acs = jnp.exp(acs)
    xd_t = (xdt * jnp.exp(acs_last - acs)).T
    cd_t = jnp.exp(acs_last).T
    lane = lax.broadcasted_iota(I32, (LANES, ll), 1)
    y_off = []
    for s in range(n_seq):
        rs = slice(s * SEQ_PAD, (s + 1) * SEQ_PAD)
        state = h0_ref[s].reshape(LANES, D_STATE)
        y_off.append(lax.dot_general(cm[rs], state, NT_DIMS, precision=HI,
                                     preferred_element_type=F32) * e_acs[rs])
        in_seq = (lane >= s * SEQ_PAD) & (lane < (s + 1) * SEQ_PAD)
        upd = jnp.dot(jnp.where(in_seq, xd_t, 0.0), bm, precision=HI, preferred_element_type=F32)
        new = state * cd_t[:, s * SEQ_PAD:s * SEQ_PAD + 1] + upd
        h1_ref[s] = new.reshape(2, SSM_HEADDIM, D_STATE)
    y_ref[...] = y + jnp.concatenate(y_off, axis=0)


def ssd_sample(xc, dt_exp, dtb_exp, a_exp, dsk_exp, h0, d_inner, t_valid, n_seq=16):
    rows = xc.shape[0]
    b = h0.shape[0]
    n_pairs = d_inner // LANES
    ppg = n_pairs // SSM_GROUPS
    ll = n_seq * SEQ_PAD
    first_b = d_inner // D_STATE
    return pl.pallas_call(
        functools.partial(_ssd_sample_body, n_seq=n_seq, t_valid=t_valid),
        grid=(b // n_seq, n_pairs),
        in_specs=[
            pl.BlockSpec((ll, LANES), lambda s, k: (s, k)),
            pl.BlockSpec((ll, D_STATE), lambda s, k: (s, first_b + k // ppg)),
            pl.BlockSpec((ll, D_STATE), lambda s, k: (s, first_b + SSM_GROUPS + k // ppg)),
            pl.BlockSpec((ll, LANES), lambda s, k: (s, k)),
            pl.BlockSpec((1, LANES), lambda s, k: (0, k)),
            pl.BlockSpec((1, LANES), lambda s, k: (0, k)),
            pl.BlockSpec((1, LANES), lambda s, k: (0, k)),
            pl.BlockSpec((n_seq, 2, SSM_HEADDIM, D_STATE), lambda s, k: (s, k, 0, 0)),
        ],
        out_specs=[
            pl.BlockSpec((ll, LANES), lambda s, k: (s, k)),
            pl.BlockSpec((n_seq, 2, SSM_HEADDIM, D_STATE), lambda s, k: (s, k, 0, 0)),
        ],
        out_shape=[
            jax.ShapeDtypeStruct((rows, d_inner), F32),
            jax.ShapeDtypeStruct(h0.shape, F32),
        ],
        compiler_params=_cparams(("arbitrary", "arbitrary"), 32),
        name="ssd_sample",
    )(xc, xc, xc, dt_exp, dtb_exp, a_exp, dsk_exp, h0)


def _gated_norm_body(y_ref, z_ref, g_ref, o_ref, *, d_inner):
    z = z_ref[...]
    yg = y_ref[...] * (z * _sigmoid(z))
    gw = d_inner // SSM_GROUPS
    for g in range(SSM_GROUPS):
        gs = slice(g * gw, (g + 1) * gw)
        v = yg[:, gs]
        r = lax.rsqrt(jnp.mean(v * v, axis=-1, keepdims=True) + EPS)
        o_ref[:, gs] = (v * r * g_ref[:, gs]).astype(BF16)


def gated_norm(y, z, norm_g, d_inner, tm=TM):
    t = y.shape[0]
    return pl.pallas_call(
        functools.partial(_gated_norm_body, d_inner=d_inner),
        grid=(t // tm,),
        in_specs=[
            pl.BlockSpec((tm, d_inner), lambda i: (i, 0)),
            pl.BlockSpec((tm, d_inner), lambda i: (i, 0)),
            _const_spec((1, d_inner)),
        ],
        out_specs=pl.BlockSpec((tm, d_inner), lambda i: (i, 0)),
        out_shape=jax.ShapeDtypeStruct((t, d_inner), BF16),
        compiler_params=_cparams(("arbitrary",), 32),
        name="gated_norm",
    )(y, z, norm_g.reshape(1, d_inner))


def _pack_cols(w, pieces, total):
    cols = [w[:, a:b] for a, b in pieces]
    used = sum(b - a for a, b in pieces)
    if total > used:
        cols.append(jnp.zeros((w.shape[0], total - used), w.dtype))
    return jnp.concatenate(cols, axis=1).astype(BF16)


def _pad_rows(a, rows, axis):
    pad = [(0, 0)] * a.ndim
    pad[axis] = (0, rows - a.shape[axis])
    return jnp.pad(a, pad)


def _dsa_layer(x_all, n_prompt, n_seq, t_s, mix_g, w_in, rel_bias, cache_k, cache_v, cache_kidx, layer,
               page_table):
    qw, kvw, qiw = N_HEADS_A * HEAD_DIM, KV_HEADS * HEAD_DIM, IDX_HEADS * IDX_DIM
    o_q, o_k, o_v, o_qi = 0, qw, qw + kvw, qw + 2 * kvw
    o_ki = o_qi + qiw
    o_wi = o_ki + IDX_DIM
    o_qm = o_wi + IDX_HEADS
    col = {"q": 0, "k": qw, "qi": qw + kvw, "v": qw + kvw + qiw, "qm": qw + 2 * kvw + qiw}
    col["kw"] = col["qm"] + MEM_WIDTH
    width = col["kw"] + LANES
    w = _pack_cols(w_in, [(o_q, o_q + qw), (o_k, o_k + kvw), (o_qi, o_qi + qiw), (o_v, o_v + kvw),
                          (o_qm, o_qm + MEM_WIDTH), (o_ki, o_ki + IDX_DIM), (o_wi, o_wi + IDX_HEADS)], width)
    z = norm_matmul(x_all, mix_g, w, TM, width // 3)

    k_all = z[:, col["k"]:col["k"] + kvw]
    v_all = z[:, col["v"]:col["v"] + kvw]
    ki_all = z[:, col["kw"]:col["kw"] + IDX_DIM]

    kiw_bf = z[:n_prompt, col["kw"]:col["kw"] + LANES].astype(BF16)
    mix_p = dsa_prompt(z, n_prompt, col, kiw_bf, k_all[:n_prompt].astype(BF16),
                       v_all[:n_prompt].astype(BF16), rel_bias)

    zs = z[n_prompt:].reshape(n_seq, t_s, width)
    zs8 = _pad_rows(zs, SEQ_PAD, 1)
    qi8 = zs8[:, :, col["qi"]:col["qi"] + qiw].reshape(n_seq, SEQ_PAD * IDX_HEADS, IDX_DIM)
    w8 = zs8[:, :, col["kw"] + IDX_DIM:col["kw"] + IDX_DIM + IDX_HEADS].reshape(n_seq, SEQ_PAD * IDX_HEADS, 1)
    q8 = zs8[:, :, :qw].reshape(n_seq, SEQ_PAD, N_HEADS_A, HEAD_DIM).transpose(0, 2, 1, 3)
    q8 = q8.reshape(n_seq, N_HEADS_A * SEQ_PAD, HEAD_DIM)
    kidx_new = _pad_rows(zs[:, :, col["kw"]:col["kw"] + IDX_DIM], PAGE_SIZE, 1)
    k_new = _pad_rows(zs[:, :, col["k"]:col["k"] + kvw], PAGE_SIZE, 1)
    v_new = _pad_rows(zs[:, :, col["v"]:col["v"] + kvw], PAGE_SIZE, 1)
    o_s = dsa_sample(qi8, w8, q8, kidx_new, k_new, v_new, cache_kidx, cache_k, cache_v, layer,
                     page_table, rel_bias, t_s)
    mix_s = o_s.reshape(n_seq, N_HEADS_A, SEQ_PAD, HEAD_DIM)[:, :, :t_s].transpose(0, 2, 1, 3)
    mix_s = mix_s.reshape(n_seq * t_s, qw).astype(BF16)
    return z, col["qm"] // MEM_WIDTH, jnp.concatenate([mix_p, mix_s], axis=0), k_all, v_all, ki_all


def _ssd_layer(x_all, n_prompt, n_seq, t_s, mix_g, w_in, conv_w, conv_b, dt_bias, a_log, d_skip, norm_g,
               state_conv, state_ssm):
    n_heads = dt_bias.shape[0]
    d_inner = n_heads * SSM_HEADDIM
    cdim = d_inner + 2 * SSM_GROUPS * D_STATE
    o_x, o_dt = d_inner, d_inner + cdim
    o_qm = o_dt + n_heads
    c_qm = d_inner + cdim
    c_dt = c_qm + MEM_WIDTH
    width = _round_up(c_dt + n_heads, 2 * LANES)
    w = _pack_cols(w_in, [(0, d_inner), (o_x, o_x + cdim), (o_qm, o_qm + MEM_WIDTH), (o_dt, o_dt + n_heads)], width)
    z = norm_matmul(x_all, mix_g, w, TM, width // 2)
    xbc = z[:, d_inner:d_inner + cdim]
    dt_exp = jnp.repeat(z[:, c_dt:c_dt + n_heads], SSM_HEADDIM, axis=1)
    dtb_exp = jnp.repeat(dt_bias.astype(F32), SSM_HEADDIM).reshape(1, d_inner)
    a_exp = -jnp.exp(jnp.repeat(a_log.astype(F32), SSM_HEADDIM)).reshape(1, d_inner)
    dsk_exp = jnp.repeat(d_skip.astype(F32), SSM_HEADDIM).reshape(1, d_inner)

    xbc_p = xbc[:n_prompt][None]
    xc_p = conv_silu(xbc_p, 0, jnp.zeros((1, SUBLANES, cdim), F32), conv_w, conv_b, SSD_L, cdim)[0]
    y_p, hf_p = ssd_prompt(xc_p, dt_exp[:n_prompt], dtb_exp, a_exp, dsk_exp, d_inner)
    new_conv_p = xbc[n_prompt - (CONV_W - 1):n_prompt]

    xbc_s = xbc[n_prompt:].reshape(n_seq, t_s, cdim)
    xbc_s8 = _pad_rows(xbc_s, SEQ_PAD, 1)
    st8 = jnp.concatenate([jnp.zeros((n_seq, SUBLANES - (CONV_W - 1), cdim), F32), state_conv.astype(F32)], axis=1)
    xc_s = conv_silu(xbc_s8, 0, st8, conv_w, conv_b, SEQ_PAD, cdim).reshape(n_seq * SEQ_PAD, cdim)
    dt_s8 = _pad_rows(dt_exp[n_prompt:].reshape(n_seq, t_s, d_inner), SEQ_PAD, 1).reshape(n_seq * SEQ_PAD, d_inner)
    y_s8, hf_s = ssd_sample(xc_s, dt_s8, dtb_exp, a_exp, dsk_exp, state_ssm.astype(F32), d_inner, t_s)
    y_s = y_s8.reshape(n_seq, SEQ_PAD, d_inner)[:, :t_s].reshape(n_seq * t_s, d_inner)
    new_conv_s = jnp.concatenate([state_conv.astype(F32), xbc_s], axis=1)[:, -(CONV_W - 1):]

    mix = gated_norm(jnp.concatenate([y_p, y_s], axis=0), z, norm_g, d_inner)
    return z, c_qm // MEM_WIDTH, mix, hf_p, new_conv_p, hf_s, new_conv_s


def kernel(x_prompt, x_sample, mem_prompt, cache_k, cache_v, cache_kidx, page_table, state_ssm, state_conv,
           cache_mem_k, cache_mem_v, rel_bias, ffn1_g, ffn1_w_gu, ffn1_w_down, mix_g, mem_g, w_mem_kv,
           w_in_attn, w_in_ssd, conv_w, conv_b, dt_bias, a_log, d_skip, ssd_norm_g, w_out,
           ffn2_g, ffn2_w_gu, ffn2_w_down, final_g):
    bp, n_prompt, d = x_prompt.shape
    n_seq, t_s, _ = x_sample.shape
    assert bp == 1
    depth = ffn1_g.shape[0]
    n_mem = mem_prompt.shape[1]
    x_all = jnp.concatenate([x_prompt[0], x_sample.reshape(n_seq * t_s, d)], axis=0)
    outs = {k: [] for k in ("pk", "pv", "pki", "pssm", "pconv", "pmk", "pmv", "sk", "sv", "ski", "sssm", "sconv")}
    y_all = None
    for i in range(depth):
        j = i // 2
        x_all = ffn(x_all, ffn1_g[i], ffn1_w_gu[i], ffn1_w_down[i])
        mkv = norm_matmul(mem_prompt[0], mem_g[i], w_mem_kv[i].astype(BF16), n_mem, 2 * MEM_WIDTH)
        outs["pmk"].append(mkv[:, :MEM_WIDTH].reshape(1, n_mem, MEM_HEADS, MEM_HEAD_DIM))
        outs["pmv"].append(mkv[:, MEM_WIDTH:].reshape(1, n_mem, MEM_HEADS, MEM_HEAD_DIM))
        if i % 2 == 0:
            z, qm_blk, mix, k_all, v_all, ki_all = _dsa_layer(
                x_all, n_prompt, n_seq, t_s, mix_g[i], w_in_attn[j], rel_bias,
                cache_k, cache_v, cache_kidx, j, page_table)
            outs["pk"].append(k_all[:n_prompt].reshape(1, n_prompt, KV_HEADS, HEAD_DIM))
            outs["pv"].append(v_all[:n_prompt].reshape(1, n_prompt, KV_HEADS, HEAD_DIM))
            outs["pki"].append(ki_all[:n_prompt].reshape(1, n_prompt, IDX_DIM))
            outs["sk"].append(k_all[n_prompt:].reshape(n_seq, t_s, KV_HEADS, HEAD_DIM))
            outs["sv"].append(v_all[n_prompt:].reshape(n_seq, t_s, KV_HEADS, HEAD_DIM))
            outs["ski"].append(ki_all[n_prompt:].reshape(n_seq, t_s, IDX_DIM))
        else:
            z, qm_blk, mix, hf_p, conv_p, hf_s, conv_s = _ssd_layer(
                x_all, n_prompt, n_seq, t_s, mix_g[i], w_in_ssd[j], conv_w[j], conv_b[j], dt_bias[j],
                a_log[j], d_skip[j], ssd_norm_g[j], state_conv[j], state_ssm[j])
            outs["pssm"].append(hf_p[None])
            outs["pconv"].append(conv_p[None])
            outs["sssm"].append(hf_s)
            outs["sconv"].append(conv_s)
        mem_p = mem_attn_prompt(z, qm_blk, n_prompt, mkv)
        qm_s = z[n_prompt:, qm_blk * MEM_WIDTH:(qm_blk + 1) * MEM_WIDTH].reshape(n_seq, t_s, MEM_WIDTH)
        mem_s = mem_attn_sample(_pad_rows(qm_s, SEQ_PAD, 1), cache_mem_k, cache_mem_v, i)
        mem = jnp.concatenate([mem_p, mem_s[:, :t_s].reshape(n_seq * t_s, MEM_WIDTH)], axis=0)
        x_all = out_proj(x_all, mix, mem, w_out[i])
        if i == depth - 1:
            x_all, y_all = ffn(x_all, ffn2_g[i], ffn2_w_gu[i], ffn2_w_down[i], final_g=final_g)
        else:
            x_all = ffn(x_all, ffn2_g[i], ffn2_w_gu[i], ffn2_w_down[i])
    y_prompt = y_all[:n_prompt][None]
    y_sample = y_all[n_prompt:].reshape(n_seq, t_s, d)
    st = lambda k: jnp.stack(outs[k])
    return (y_prompt, y_sample, st("pk"), st("pv"), st("pki"), st("pssm"), st("pconv"), st("pmk"), st("pmv"),
            st("sk"), st("sv"), st("ski"), st("sssm"), st("sconv"))
```

```python
import functools
import math

import numpy as np
import jax
import jax.numpy as jnp
from jax import lax
from jax.experimental import pallas as pl
from jax.experimental.pallas import tpu as pltpu

F32, BF16, I32 = jnp.float32, jnp.bfloat16, jnp.int32
HI = lax.Precision.HIGHEST
NT_DIMS = (((1,), (1,)), ((), ()))

HEAD_DIM = 128
KV_HEADS = 4
Q_PER_KV = 3
N_HEADS_A = KV_HEADS * Q_PER_KV
IDX_HEADS = 16
IDX_DIM = 64
TOPK_MAX = 256
REL_BUCKETS = 32
REL_MAX_EXACT = 16
REL_MAX_DIST = 128
MEM_HEADS = 4
MEM_HEAD_DIM = 128
MEM_WIDTH = MEM_HEADS * MEM_HEAD_DIM
SSM_HEADDIM = 64
SSM_GROUPS = 4
D_STATE = 128
CONV_W = 4
PAGE_SIZE = 128
EPS = 1e-6

LANES = 128
SUBLANES = 8
NEG = -1e30
INT_MIN = -2 ** 31
IDX_SCALE = IDX_DIM ** -0.5 * IDX_HEADS ** -0.5
ATT_SCALE = HEAD_DIM ** -0.5
MEM_SCALE = MEM_HEAD_DIM ** -0.5

QB = 128
KC = 512
TM = 512
TF = 512
SSD_L = 128
SEQ_PAD = 8


def _cparams(sem, vmem_mb):
    return pltpu.CompilerParams(dimension_semantics=sem, vmem_limit_bytes=vmem_mb * 2 ** 20)


def _round_up(n, m):
    return (n + m - 1) // m * m


def _const_spec(shape):
    nd = len(shape)
    return pl.BlockSpec(shape, lambda *_: (0,) * nd)


def _resident_spec(shape):
    nd = len(shape)
    return pl.BlockSpec(shape, lambda *_: (0,) * nd, pipeline_mode=pl.Buffered(1))


def _rms(x, g):
    return x * lax.rsqrt(jnp.mean(x * x, axis=-1, keepdims=True) + EPS) * g


def _sigmoid(x):
    return 1.0 / (1.0 + jnp.exp(-x))


def _softplus(x):
    return jnp.maximum(x, 0.0) + jnp.log1p(jnp.exp(-jnp.abs(x)))


def _ffn_body(x_ref, g_ref, wg_ref, wu_ref, wd_ref, *rest, nj, with_final):
    if with_final:
        fg_ref, o_ref, y_ref, xn_ref, acc_ref = rest
    else:
        o_ref, xn_ref, acc_ref = rest
    j = pl.program_id(1)

    @pl.when(j == 0)
    def _():
        xn_ref[...] = _rms(x_ref[...], g_ref[...]).astype(BF16)
        acc_ref[...] = jnp.zeros_like(acc_ref)

    xn = xn_ref[...]
    gate = jnp.dot(xn, wg_ref[...], preferred_element_type=F32)
    up = jnp.dot(xn, wu_ref[...], preferred_element_type=F32)
    a = gate * _sigmoid(gate) * up
    acc_ref[...] += jnp.dot(a.astype(BF16), wd_ref[...], preferred_element_type=F32)

    @pl.when(j == nj - 1)
    def _():
        o = x_ref[...] + 0.5 * acc_ref[...]
        o_ref[...] = o
        if with_final:
            y_ref[...] = _rms(o, fg_ref[...])


def _prep_ffn(w_gu, w_d, tf):
    ff = w_gu.shape[1] // 2
    ffp = _round_up(ff, tf)
    pad = ((0, 0), (0, ffp - ff))
    wg = jnp.pad(w_gu[:, :ff].astype(BF16), pad)
    wu = jnp.pad(w_gu[:, ff:].astype(BF16), pad)
    wd = jnp.pad(w_d.astype(BF16), ((0, ffp - ff), (0, 0)))
    return wg, wu, wd, ffp // tf


def ffn(x, g, w_gu, w_d, final_g=None, tm=TM, tf=TF):
    t, d = x.shape
    wg, wu, wd, nj = _prep_ffn(w_gu, w_d, tf)
    with_final = final_g is not None
    in_specs = [
        pl.BlockSpec((tm, d), lambda i, j: (i, 0)),
        _const_spec((1, d)),
        pl.BlockSpec((d, tf), lambda i, j: (0, j)),
        pl.BlockSpec((d, tf), lambda i, j: (0, j)),
        pl.BlockSpec((tf, d), lambda i, j: (j, 0)),
    ]
    args = [x, g.reshape(1, d), wg, wu, wd]
    out_shape = [jax.ShapeDtypeStruct((t, d), F32)]
    out_specs = [pl.BlockSpec((tm, d), lambda i, j: (i, 0))]
    if with_final:
        in_specs.append(_const_spec((1, d)))
        args.append(final_g.reshape(1, d))
        out_shape.append(jax.ShapeDtypeStruct((t, d), F32))
        out_specs.append(pl.BlockSpec((tm, d), lambda i, j: (i, 0)))
    res = pl.pallas_call(
        functools.partial(_ffn_body, nj=nj, with_final=with_final),
        grid=(t // tm, nj),
        in_specs=in_specs,
        out_specs=out_specs,
        out_shape=out_shape,
        scratch_shapes=[pltpu.VMEM((tm, d), BF16), pltpu.VMEM((tm, d), F32)],
        compiler_params=_cparams(("arbitrary", "arbitrary"), 56),
        name="ffn",
    )(*args)
    return res if with_final else res[0]


def _norm_matmul_body(x_ref, g_ref, w_ref, o_ref, xn_ref):
    @pl.when(pl.program_id(1) == 0)
    def _():
        xn_ref[...] = _rms(x_ref[...], g_ref[...]).astype(BF16)

    o_ref[...] = jnp.dot(xn_ref[...], w_ref[...], preferred_element_type=F32)


def norm_matmul(x, g, w_bf16, tm, tn):
    t, d = x.shape
    n = w_bf16.shape[1]
    return pl.pallas_call(
        _norm_matmul_body,
        grid=(t // tm, n // tn),
        in_specs=[
            pl.BlockSpec((tm, d), lambda i, j: (i, 0)),
            _const_spec((1, d)),
            pl.BlockSpec((d, tn), lambda i, j: (0, j)),
        ],
        out_specs=pl.BlockSpec((tm, tn), lambda i, j: (i, j)),
        out_shape=jax.ShapeDtypeStruct((t, n), F32),
        scratch_shapes=[pltpu.VMEM((tm, d), BF16)],
        compiler_params=_cparams(("arbitrary", "arbitrary"), 48),
        name="norm_matmul",
    )(x, g.reshape(1, d), w_bf16)


def _out_proj_body(x_ref, mix_ref, mem_ref, w1_ref, w2_ref, o_ref):
    o_ref[...] = (x_ref[...]
                  + jnp.dot(mix_ref[...], w1_ref[...], preferred_element_type=F32)
                  + jnp.dot(mem_ref[...], w2_ref[...], preferred_element_type=F32))


def out_proj(x, mix, mem, w_out, tm=TM):
    t, d = x.shape
    dm, dw = mix.shape[1], mem.shape[1]
    w1 = w_out[:dm].astype(BF16)
    w2 = w_out[dm:].astype(BF16)
    return pl.pallas_call(
        _out_proj_body,
        grid=(t // tm,),
        in_specs=[
            pl.BlockSpec((tm, d), lambda i: (i, 0)),
            pl.BlockSpec((tm, dm), lambda i: (i, 0)),
            pl.BlockSpec((tm, dw), lambda i: (i, 0)),
            _const_spec((dm, d)),
            _const_spec((dw, d)),
        ],
        out_specs=pl.BlockSpec((tm, d), lambda i: (i, 0)),
        out_shape=jax.ShapeDtypeStruct((t, d), F32),
        compiler_params=_cparams(("arbitrary",), 48),
        name="out_proj",
    )(x, mix, mem, w1, w2)


def _softmax_rows(s):
    m = jnp.max(s, axis=-1, keepdims=True)
    p = jnp.exp(s - m)
    return p, jnp.sum(p, axis=-1, keepdims=True)


def _mem_attn_prompt_body(q_ref, mk_ref, mv_ref, o_ref):
    for h in range(MEM_HEADS):
        sl = slice(h * MEM_HEAD_DIM, (h + 1) * MEM_HEAD_DIM)
        q = (q_ref[:, sl] * MEM_SCALE).astype(BF16)
        s = lax.dot_general(q, mk_ref[:, sl].astype(BF16), NT_DIMS, preferred_element_type=F32)
        p, l = _softmax_rows(s)
        o = jnp.dot(p.astype(BF16), mv_ref[:, sl].astype(BF16), preferred_element_type=F32)
        o_ref[:, sl] = (o / l).astype(BF16)


def mem_attn_prompt(z, qm_col_block, n_rows, mkv, tm=TM):
    m = mkv.shape[0]
    return pl.pallas_call(
        _mem_attn_prompt_body,
        grid=(n_rows // tm,),
        in_specs=[
            pl.BlockSpec((tm, MEM_WIDTH), lambda i: (i, qm_col_block)),
            pl.BlockSpec((m, MEM_WIDTH), lambda i: (0, 0)),
            pl.BlockSpec((m, MEM_WIDTH), lambda i: (0, 1)),
        ],
        out_specs=pl.BlockSpec((tm, MEM_WIDTH), lambda i: (i, 0)),
        out_shape=jax.ShapeDtypeStruct((n_rows, MEM_WIDTH), BF16),
        compiler_params=_cparams(("arbitrary",), 32),
        name="mem_attn_prompt",
    )(z, mkv, mkv)


def _mem_attn_sample_body(q_ref, k_ref, v_ref, o_ref, *, bb, m):
    rows, cols = MEM_HEADS * SEQ_PAD, m * MEM_HEADS
    row_head = lax.shift_right_logical(lax.broadcasted_iota(I32, (rows, cols), 0), SEQ_PAD.bit_length() - 1)
    col_head = lax.broadcasted_iota(I32, (rows, cols), 1) & (MEM_HEADS - 1)
    madd = jnp.where(row_head == col_head, 0.0, NEG)
    for b in range(bb):
        q = jnp.concatenate([q_ref[b, :, h * MEM_HEAD_DIM:(h + 1) * MEM_HEAD_DIM] for h in range(MEM_HEADS)], axis=0)
        s = lax.dot_general((q * MEM_SCALE).astype(BF16), k_ref[b].astype(BF16), NT_DIMS,
                            preferred_element_type=F32)
        p, l = _softmax_rows(s + madd)
        o = jnp.dot(p.astype(BF16), v_ref[b].astype(BF16), preferred_element_type=F32) / l
        for h in range(MEM_HEADS):
            o_ref[b, :, h * MEM_HEAD_DIM:(h + 1) * MEM_HEAD_DIM] = o[h * SEQ_PAD:(h + 1) * SEQ_PAD].astype(BF16)


def mem_attn_sample(qm8, mem_k, mem_v, layer, bb=8):
    depth, b, m, nh, hd = mem_k.shape
    mem_k = mem_k.reshape(depth, b, m * nh, hd)
    mem_v = mem_v.reshape(depth, b, m * nh, hd)
    cache_spec = pl.BlockSpec((None, bb, m * nh, hd), lambda i: (layer, i, 0, 0))
    return pl.pallas_call(
        functools.partial(_mem_attn_sample_body, bb=bb, m=m),
        grid=(b // bb,),
        in_specs=[pl.BlockSpec((bb, SEQ_PAD, MEM_WIDTH), lambda i: (i, 0, 0)), cache_spec, cache_spec],
        out_specs=pl.BlockSpec((bb, SEQ_PAD, MEM_WIDTH), lambda i: (i, 0, 0)),
        out_shape=jax.ShapeDtypeStruct((b, SEQ_PAD, MEM_WIDTH), BF16),
        compiler_params=_cparams(("arbitrary",), 40),
        name="mem_attn_sample",
    )(qm8, mem_k, mem_v)


def _sortable_key(x):
    b = pltpu.bitcast(x, I32)
    return jnp.where(b < 0, (b ^ 0x7FFFFFFF) + 1, b)


def _t5_bucket_np(dist):
    n = np.maximum(dist, 0)
    nf = np.maximum(n, 1).astype(np.float64)
    large = REL_MAX_EXACT + (np.log(nf / REL_MAX_EXACT) / math.log(REL_MAX_DIST / REL_MAX_EXACT)
                             * (REL_BUCKETS - REL_MAX_EXACT)).astype(np.int32)
    large = np.minimum(large, REL_BUCKETS - 1)
    return np.where(n < REL_MAX_EXACT, n, large)


FAR_DIST = int(np.min(np.nonzero(_t5_bucket_np(np.arange(4 * REL_MAX_DIST)) == REL_BUCKETS - 1)[0]))
assert np.all(_t5_bucket_np(np.arange(FAR_DIST, 1 << 16)) == REL_BUCKETS - 1) and FAR_DIST <= QB


def _toeplitz_bias(rel_bias, n, m, k0):
    p = n + m
    d = k0 + (n - 1) - np.arange(p)
    b = _t5_bucket_np(d)
    keep = (d >= 0) & (b != REL_BUCKETS - 1)
    u = jnp.where(keep[:, None], rel_bias[b] - rel_bias[REL_BUCKETS - 1], 0.0).T
    skew = jnp.tile(u, (1, n))[:, :n * (p - 1)].reshape(-1, n, p - 1)
    return skew[:, :, n - 1:n - 1 + m].astype(F32)


def _threshold_search(count_ge, count_tie_lt, shape, topk, n_idx_bits, all_idx):
    zero = jnp.zeros(shape, I32)
    t0 = jnp.where(count_ge(zero) >= topk, zero, jnp.full(shape, INT_MIN, I32))

    def bit_body(b, t):
        cand = t | lax.shift_left(jnp.int32(1), 30 - b)
        return jnp.where(count_ge(cand) >= topk, cand, t)

    t = lax.fori_loop(0, 31, bit_body, t0)
    has_k = t > INT_MIN
    n_ge = count_ge(t)
    need = topk - count_ge(t + 1)
    excess = jnp.max(jnp.where(has_k, n_ge, 0.0)) > topk

    def tie_search(_):
        def jbit(b, j):
            cand = j | lax.shift_left(jnp.int32(1), n_idx_bits - 1 - b)
            return jnp.where(count_tie_lt(t, cand) < need, cand, j)
        return lax.fori_loop(0, n_idx_bits, jbit, jnp.zeros(shape, I32))

    j = lax.cond(excess, tie_search, lambda _: jnp.full(shape, all_idx, I32), 0)
    j = jnp.where(has_k, j, -1)
    return t, j


def _dsa_prompt_body(q_ref, qi_ref, kw_ref, kiw_ref, k_ref, v_ref, bias_ref, o_ref,
                     rhs_ref, wt_ref, key_ref, qe_ref, m_ref, acc_ref, p_ref, *, seq, topk):
    i = pl.program_id(0)
    q0 = i * QB

    qi_t = qi_ref[...].T
    pad = jnp.zeros((LANES - IDX_DIM, QB), F32)
    for h in range(IDX_HEADS):
        blk = jnp.concatenate([qi_t[h * IDX_DIM:(h + 1) * IDX_DIM], pad], axis=0)
        rhs_ref[:, h * QB:(h + 1) * QB] = blk.astype(BF16)
    wt_ref[...] = kw_ref[...].T * IDX_SCALE

    n_chunks = (jnp.maximum(i + 1, 2) * QB + KC - 1) // KC

    def score_chunk(c, carry):
        c0 = pl.multiple_of(c * KC, KC)
        x = jnp.dot(kiw_ref[pl.ds(c0, KC), :], rhs_ref[...], preferred_element_type=F32)
        acc = jnp.zeros((KC, QB), F32)
        for h in range(IDX_HEADS):
            acc = acc + jnp.maximum(x[:, h * QB:(h + 1) * QB], 0.0) * wt_ref[IDX_DIM + h:IDX_DIM + h + 1, :]
        kpos = c0 + lax.broadcasted_iota(I32, (KC, QB), 0)
        qpos = q0 + lax.broadcasted_iota(I32, (KC, QB), 1)
        key_ref[pl.ds(c0, KC), :] = jnp.where(kpos <= qpos, _sortable_key(acc), INT_MIN)
        return carry

    lax.fori_loop(0, n_chunks, score_chunk, 0)

    def column_count(hit_of_chunk):
        acc_rows = 8 * SUBLANES

        def body(c, a):
            c0 = pl.multiple_of(c * KC, KC)
            hit = jnp.where(hit_of_chunk(key_ref[pl.ds(c0, KC), :], c0), 1.0, 0.0)
            return a + hit.reshape(KC // acc_rows, acc_rows, QB).sum(axis=0)

        a = lax.fori_loop(0, n_chunks, body, jnp.zeros((acc_rows, QB), F32))
        return jnp.sum(a, axis=0, keepdims=True)

    def count_ge(cand):
        return column_count(lambda kk, c0: kk >= cand)

    def count_tie_lt(t, jc):
        row = lax.broadcasted_iota(I32, (KC, QB), 0)
        return column_count(lambda kk, c0: (kk == t) & (c0 + row < jc))

    t, j = _threshold_search(count_ge, count_tie_lt, (1, QB), topk, (seq - 1).bit_length(), seq)

    eye = jnp.where(lax.broadcasted_iota(I32, (QB, QB), 0) == lax.broadcasted_iota(I32, (QB, QB), 1),
                    1.0, 0.0).astype(BF16)
    for g in range(KV_HEADS):
        for r in range(Q_PER_KV):
            hq = g * Q_PER_KV + r
            rs = slice(r * QB, (r + 1) * QB)
            qe_ref[g, rs, 0:HEAD_DIM] = (q_ref[:, hq * HEAD_DIM:(hq + 1) * HEAD_DIM] * ATT_SCALE).astype(BF16)
            qe_ref[g, rs, HEAD_DIM:2 * HEAD_DIM] = eye
    m_ref[...] = jnp.full(m_ref.shape, 3 * NEG, F32)
    acc_ref[...] = jnp.zeros(acc_ref.shape, F32)

    def mask_t(c0, width, far_end):
        kk = key_ref[pl.ds(c0, width), :]
        kpos = c0 + lax.broadcasted_iota(I32, (width, QB), 0)
        sel = (kk > t) | ((kk == t) & (kpos <= j))
        if far_end is not None:
            sel = sel & (kpos < far_end)
        return jnp.where(sel, 0.0, NEG).astype(BF16)

    def add_pv(g, p, c0, width):
        gs = slice(g * HEAD_DIM, (g + 1) * HEAD_DIM)
        v_ext = jnp.concatenate([v_ref[pl.ds(c0, width), gs], jnp.ones((width, HEAD_DIM), BF16)], axis=1)
        acc_ref[g] = acc_ref[g] + jnp.dot(p, v_ext, preferred_element_type=F32)

    def probs(g, c0, width, madd_t, bias):
        gs = slice(g * HEAD_DIM, (g + 1) * HEAD_DIM)
        k_ext = jnp.concatenate([k_ref[pl.ds(c0, width), gs], madd_t], axis=1)
        s = lax.dot_general(qe_ref[g], k_ext, NT_DIMS, preferred_element_type=F32)
        if bias is not None:
            s = s + bias
        m_old = m_ref[g]
        m_new = jnp.maximum(m_old, jnp.max(s, axis=1, keepdims=True))
        alpha = jnp.exp(m_old - m_new)
        acc_ref[g] = jnp.concatenate([alpha, alpha], axis=1) * acc_ref[g]
        m_ref[g] = m_new
        return jnp.exp((s - m_new[:, 0:1]).astype(BF16))

    far_end = jnp.maximum(i - 1, 0) * QB
    n_far = (far_end + KC - 1) // KC
    p_ref[...] = jnp.zeros(p_ref.shape, BF16)

    def far_body(c, carry):
        c0 = pl.multiple_of(c * KC, KC)
        prev0 = pl.multiple_of(jnp.maximum(c - 1, 0) * KC, KC)
        madd_t = mask_t(c0, KC, far_end)
        for g in range(KV_HEADS):
            add_pv(g, p_ref[g], prev0, KC)
            p_ref[g] = probs(g, c0, KC, madd_t, None)
        return carry

    lax.fori_loop(0, n_far, far_body, 0)
    last0 = pl.multiple_of(jnp.maximum(n_far - 1, 0) * KC, KC)
    near0 = pl.multiple_of(far_end, QB)
    variant = jnp.minimum(i, 1)
    madd_t = mask_t(near0, 2 * QB, None)
    for g in range(KV_HEADS):
        add_pv(g, p_ref[g], last0, KC)
        add_pv(g, probs(g, near0, 2 * QB, madd_t, bias_ref[variant, g]), near0, 2 * QB)

    for g in range(KV_HEADS):
        o = acc_ref[g, :, 0:HEAD_DIM] / acc_ref[g, :, HEAD_DIM:2 * HEAD_DIM]
        for r in range(Q_PER_KV):
            hq = g * Q_PER_KV + r
            o_ref[:, hq * HEAD_DIM:(hq + 1) * HEAD_DIM] = o[r * QB:(r + 1) * QB].astype(BF16)


def _prompt_bias(rel_bias):
    out = [_toeplitz_bias(rel_bias, QB, 2 * QB, k0).reshape(KV_HEADS, Q_PER_KV * QB, 2 * QB) for k0 in (0, QB)]
    return jnp.stack(out)


def dsa_prompt(z, seq, col, kiw_bf, k_bf, v_bf, rel_bias):
    topk = min(TOPK_MAX, seq // 4)
    assert seq % KC == 0 and seq >= 2 * QB
    qw = N_HEADS_A * HEAD_DIM
    qiw = IDX_HEADS * IDX_DIM
    bias = _prompt_bias(rel_bias)
    rows = Q_PER_KV * QB
    return pl.pallas_call(
        functools.partial(_dsa_prompt_body, seq=seq, topk=topk),
        grid=(seq // QB,),
        in_specs=[
            pl.BlockSpec((QB, qw), lambda i: (i, col["q"] // qw)),
            pl.BlockSpec((QB, qiw), lambda i: (i, col["qi"] // qiw)),
            pl.BlockSpec((QB, LANES), lambda i: (i, col["kw"] // LANES)),
            _resident_spec((seq, LANES)),
            _resident_spec((seq, KV_HEADS * HEAD_DIM)),
            _resident_spec((seq, KV_HEADS * HEAD_DIM)),
            _resident_spec(bias.shape),
        ],
        out_specs=pl.BlockSpec((QB, qw), lambda i: (i, 0)),
        out_shape=jax.ShapeDtypeStruct((seq, qw), BF16),
        scratch_shapes=[
            pltpu.VMEM((LANES, IDX_HEADS * QB), BF16),
            pltpu.VMEM((LANES, QB), F32),
            pltpu.VMEM((seq, QB), I32),
            pltpu.VMEM((KV_HEADS, rows, 2 * HEAD_DIM), BF16),
            pltpu.VMEM((KV_HEADS, rows, LANES), F32),
            pltpu.VMEM((KV_HEADS, rows, 2 * HEAD_DIM), F32),
            pltpu.VMEM((KV_HEADS, rows, KC), BF16),
        ],
        compiler_params=_cparams(("arbitrary",), 56),
        name="dsa_prompt",
    )(z, z, z, kiw_bf, k_bf, v_bf, bias)


def _dsa_sample_body(pt_ref, qi_ref, w_ref, q_ref, bias_ref, *refs, n_pages, t_valid, topk):
    del pt_ref
    np1 = n_pages + 1
    kidx_refs, k_refs, v_refs = refs[0:np1], refs[np1:2 * np1], refs[2 * np1:3 * np1]
    o_ref, key_ref, s_ref = refs[3 * np1:]
    past = n_pages * PAGE_SIZE
    width = np1 * PAGE_SIZE
    rows = KV_HEADS * Q_PER_KV * SEQ_PAD

    qi = qi_ref[...].astype(BF16)
    w = w_ref[...] * IDX_SCALE
    for p in range(np1):
        x = jnp.dot(qi, kidx_refs[p][...].astype(BF16), preferred_element_type=F32)
        sc = (jnp.maximum(x, 0.0) * w).reshape(SEQ_PAD, IDX_HEADS, PAGE_SIZE).sum(axis=1)
        kpos = p * PAGE_SIZE + lax.broadcasted_iota(I32, (SEQ_PAD, PAGE_SIZE), 1)
        qpos = past + lax.broadcasted_iota(I32, (SEQ_PAD, PAGE_SIZE), 0)
        ok = (kpos <= qpos) & (kpos < past + t_valid)
        key_ref[:, p * PAGE_SIZE:(p + 1) * PAGE_SIZE] = jnp.where(ok, _sortable_key(sc), INT_MIN)

    keys = key_ref[...]
    kpos = lax.broadcasted_iota(I32, (SEQ_PAD, width), 1)

    def count_ge(cand):
        return jnp.sum(jnp.where(keys >= cand, 1.0, 0.0), axis=1, keepdims=True)

    def count_tie_lt(t, jc):
        return jnp.sum(jnp.where((keys == t) & (kpos < jc), 1.0, 0.0), axis=1, keepdims=True)

    t, j = _threshold_search(count_ge, count_tie_lt, (SEQ_PAD, 1), topk, (width - 1).bit_length(), width)
    sel = (keys > t) | ((keys == t) & (kpos <= j))
    madd = jnp.where(sel, 0.0, NEG)
    madd = jnp.concatenate([madd] * (KV_HEADS * Q_PER_KV), axis=0)

    def kv_tile(page_refs, p, g):
        if p < n_pages:
            return page_refs[p][pl.ds(g, PAGE_SIZE, stride=KV_HEADS), :].astype(BF16)
        return page_refs[p][:, g * HEAD_DIM:(g + 1) * HEAD_DIM].astype(BF16)

    grp_rows = Q_PER_KV * SEQ_PAD
    for g in range(KV_HEADS):
        rs = slice(g * grp_rows, (g + 1) * grp_rows)
        qg = (q_ref[rs, :] * ATT_SCALE).astype(BF16)
        for p in range(np1):
            s_ref[rs, p * PAGE_SIZE:(p + 1) * PAGE_SIZE] = lax.dot_general(
                qg, kv_tile(k_refs, p, g), NT_DIMS, preferred_element_type=F32)
    pr, l = _softmax_rows(s_ref[...] + bias_ref[...] + madd)
    for g in range(KV_HEADS):
        rs = slice(g * grp_rows, (g + 1) * grp_rows)
        acc = jnp.zeros((grp_rows, HEAD_DIM), F32)
        for p in range(np1):
            acc = acc + jnp.dot(pr[rs, p * PAGE_SIZE:(p + 1) * PAGE_SIZE].astype(BF16),
                                kv_tile(v_refs, p, g), preferred_element_type=F32)
        o_ref[rs, :] = acc / l[rs]


def _sample_bias(rel_bias, past, width):
    near = width - (past - PAGE_SIZE)
    b = _toeplitz_bias(rel_bias, SEQ_PAD, near, PAGE_SIZE).reshape(N_HEADS_A * SEQ_PAD, near)
    return jnp.pad(b, ((0, 0), (width - near, 0)))


def dsa_sample(qi8, w8, q8, kidx_new, k_new, v_new, cache_kidx, cache_k, cache_v, layer, page_table,
               rel_bias, t_valid):
    b, n_pages = page_table.shape
    np1 = n_pages + 1
    past = n_pages * PAGE_SIZE
    width = np1 * PAGE_SIZE
    topk = min(TOPK_MAX, (past + t_valid) // 4)
    bias = _sample_bias(rel_bias, past, width)
    rows = N_HEADS_A * SEQ_PAD
    kvw = KV_HEADS * HEAD_DIM

    def page_spec(tail, p):
        zeros = (0,) * len(tail)
        return pl.BlockSpec((None, None) + tail, lambda i, pt: (layer, pt[i, p]) + zeros)

    def new_spec(tail):
        return pl.BlockSpec((None,) + tail, lambda i, pt: (i, 0, 0))

    in_specs = [
        pl.BlockSpec((None, SEQ_PAD * IDX_HEADS, IDX_DIM), lambda i, pt: (i, 0, 0)),
        pl.BlockSpec((None, SEQ_PAD * IDX_HEADS, 1), lambda i, pt: (i, 0, 0)),
        pl.BlockSpec((None, rows, HEAD_DIM), lambda i, pt: (i, 0, 0)),
        pl.BlockSpec((rows, width), lambda i, pt: (0, 0)),
    ]
    args = [qi8, w8, q8, bias]
    kv_tail = (PAGE_SIZE * KV_HEADS, HEAD_DIM)
    cache_k = cache_k.reshape(cache_k.shape[:2] + kv_tail)
    cache_v = cache_v.reshape(cache_v.shape[:2] + kv_tail)
    kidx_tail = (IDX_DIM, PAGE_SIZE)
    cache_kidx = jnp.swapaxes(cache_kidx, 2, 3)
    kidx_new = jnp.swapaxes(kidx_new, 1, 2)
    new_tail = (PAGE_SIZE, kvw)
    for arr_cache, arr_new, tail, ntail in ((cache_kidx, kidx_new, kidx_tail, kidx_tail),
                                            (cache_k, k_new, kv_tail, new_tail), (cache_v, v_new, kv_tail, new_tail)):
        in_specs += [page_spec(tail, p) for p in range(n_pages)] + [new_spec(ntail)]
        args += [arr_cache] * n_pages + [arr_new]
    return pl.pallas_call(
        functools.partial(_dsa_sample_body, n_pages=n_pages, t_valid=t_valid, topk=topk),
        grid_spec=pltpu.PrefetchScalarGridSpec(
            num_scalar_prefetch=1,
            grid=(b,),
            in_specs=in_specs,
            out_specs=pl.BlockSpec((None, rows, HEAD_DIM), lambda i, pt: (i, 0, 0)),
            scratch_shapes=[pltpu.VMEM((SEQ_PAD, width), I32), pltpu.VMEM((rows, width), F32)],
        ),
        out_shape=jax.ShapeDtypeStruct((b, rows, HEAD_DIM), F32),
        compiler_params=_cparams(("arbitrary",), 48),
        name="dsa_sample",
    )(page_table, *args)


def _conv_body(x_ref, st_ref, w_ref, b_ref, o_ref, tail_ref, *, rows):
    @pl.when(pl.program_id(1) == 0)
    def _():
        tail_ref[...] = st_ref[...]

    x = x_ref[...]
    xc = jnp.concatenate([tail_ref[...], x], axis=0)
    out = b_ref[...] + x * w_ref[CONV_W - 1:CONV_W, :]
    for k in range(1, CONV_W):
        shifted = pltpu.roll(xc, k, 0)[SUBLANES:SUBLANES + rows]
        out = out + shifted * w_ref[CONV_W - 1 - k:CONV_W - k, :]
    o_ref[...] = out * _sigmoid(out)
    tail_ref[...] = x[rows - SUBLANES:rows]


def conv_silu(z3, col_block, state8, conv_w, conv_b, rows, cdim):
    b, t, _ = z3.shape
    return pl.pallas_call(
        functools.partial(_conv_body, rows=rows),
        grid=(b, t // rows),
        in_specs=[
            pl.BlockSpec((None, rows, cdim), lambda i, c: (i, c, col_block)),
            pl.BlockSpec((None, SUBLANES, cdim), lambda i, c: (i, 0, 0)),
            _const_spec((CONV_W, cdim)),
            _const_spec((1, cdim)),
        ],
        out_specs=pl.BlockSpec((None, rows, cdim), lambda i, c: (i, c, 0)),
        out_shape=jax.ShapeDtypeStruct((b, t, cdim), F32),
        scratch_shapes=[pltpu.VMEM((SUBLANES, cdim), F32)],
        compiler_params=_cparams(("arbitrary", "arbitrary"), 32),
        name="conv_silu",
    )(z3, state8, conv_w, conv_b.reshape(1, cdim))


def _ssd_pair(xs, dt_raw, dtb, a, dskip, cb, lmask, valid):
    dt = _softplus(dt_raw + dtb)
    if valid is not None:
        dt = jnp.where(valid, dt, 0.0)
    xdt = xs * dt
    acs = jnp.dot(jnp.where(lmask, 1.0, 0.0), dt * a, precision=HI, preferred_element_type=F32)
    acs_t = acs.T
    half = SSM_HEADDIM
    xdt_bf = xdt.astype(BF16)
    yd = []
    for lane0 in (0, half):
        seg = acs[:, lane0:lane0 + 1] - acs_t[lane0:lane0 + 1, :]
        lm = jnp.where(lmask, jnp.exp(jnp.where(lmask, seg, 0.0)), 0.0)
        yd.append(jnp.dot((cb * lm).astype(BF16), xdt_bf, preferred_element_type=F32))
    lane = lax.broadcasted_iota(I32, xs.shape, 1)
    y = jnp.where(lane < half, yd[0], yd[1]) + dskip * xs
    return xdt, acs, y


def _ssd_prompt_body(xs_ref, bm_ref, cm_ref, dt_ref, dtb_ref, a_ref, dsk_ref, y_ref, hf_ref, st_ref,
                     *, n_pairs, n_chunks):
    c = pl.program_id(0)

    @pl.when(c == 0)
    def _():
        st_ref[...] = jnp.zeros_like(st_ref)

    ll = SSD_L
    li = lax.broadcasted_iota(I32, (ll, ll), 0)
    si = lax.broadcasted_iota(I32, (ll, ll), 1)
    lmask = si <= li
    pairs_per_group = n_pairs // SSM_GROUPS
    for g in range(SSM_GROUPS):
        gs = slice(g * D_STATE, (g + 1) * D_STATE)
        bm = bm_ref[:, gs].astype(BF16)
        cm = cm_ref[:, gs].astype(BF16)
        cb = lax.dot_general(cm, bm, NT_DIMS, preferred_element_type=F32)
        for kk in range(pairs_per_group):
            k = g * pairs_per_group + kk
            ks = slice(k * LANES, (k + 1) * LANES)
            xdt, acs, y = _ssd_pair(xs_ref[:, ks], dt_ref[:, ks], dtb_ref[:, ks], a_ref[:, ks],
                                    dsk_ref[:, ks], cb, lmask, None)
            acs_last = acs[ll - 1:ll, :]
            state = st_ref[k]
            y_off = lax.dot_general(cm, state.astype(BF16), NT_DIMS, preferred_element_type=F32)
            y_ref[:, ks] = y + y_off * jnp.exp(acs)
            xd_t = (xdt * jnp.exp(acs_last - acs)).T
            upd = jnp.dot(xd_t.astype(BF16), bm, preferred_element_type=F32)
            cd = jnp.exp(jnp.broadcast_to(acs_last, (ll, LANES))).T[:, 0:1]
            st_ref[k] = state * cd + upd

    @pl.when(c == n_chunks - 1)
    def _():
        hf_ref[...] = st_ref[...]


def ssd_prompt(xc, dt_exp, dtb_exp, a_exp, dsk_exp, d_inner):
    t = xc.shape[0]
    n_pairs = d_inner // LANES
    gn = SSM_GROUPS * D_STATE
    n_chunks = t // SSD_L
    y, hf = pl.pallas_call(
        functools.partial(_ssd_prompt_body, n_pairs=n_pairs, n_chunks=n_chunks),
        grid=(n_chunks,),
        in_specs=[
            pl.BlockSpec((SSD_L, d_inner), lambda c: (c, 0)),
            pl.BlockSpec((SSD_L, gn), lambda c: (c, d_inner // gn)),
            pl.BlockSpec((SSD_L, gn), lambda c: (c, d_inner // gn + 1)),
            pl.BlockSpec((SSD_L, d_inner), lambda c: (c, 0)),
            _const_spec((1, d_inner)),
            _const_spec((1, d_inner)),
            _const_spec((1, d_inner)),
        ],
        out_specs=[
            pl.BlockSpec((SSD_L, d_inner), lambda c: (c, 0)),
            _const_spec((n_pairs, LANES, D_STATE)),
        ],
        out_shape=[
            jax.ShapeDtypeStruct((t, d_inner), F32),
            jax.ShapeDtypeStruct((n_pairs, LANES, D_STATE), F32),
        ],
        scratch_shapes=[pltpu.VMEM((n_pairs, LANES, D_STATE), F32)],
        compiler_params=_cparams(("arbitrary",), 32),
        name="ssd_prompt",
    )(xc, xc, xc, dt_exp, dtb_exp, a_exp, dsk_exp)
    return y, hf.reshape(2 * n_pairs, SSM_HEADDIM, D_STATE)


def _ssd_sample_body(xs_ref, bm_ref, cm_ref, dt_ref, dtb_ref, a_ref, dsk_ref, h0_ref, y_ref, h1_ref,
                     *, n_seq, t_valid):
    ll = n_seq * SEQ_PAD
    li = lax.broadcasted_iota(I32, (ll, ll), 0)
    si = lax.broadcasted_iota(I32, (ll, ll), 1)
    same = lax.shift_right_logical(li, 3) == lax.shift_right_logical(si, 3)
    lmask = same & (si <= li)
    last = same & ((si & (SEQ_PAD - 1)) == SEQ_PAD - 1)
    rowi = lax.broadcasted_iota(I32, (ll, LANES), 0)
    valid = (rowi & (SEQ_PAD - 1)) < t_valid
    bm = bm_ref[...].astype(BF16)
    cm = cm_ref[...].astype(BF16)
    cb = lax.dot_general(cm, bm, NT_DIMS, preferred_element_type=F32)
    xdt, acs, y = _ssd_pair(xs_ref[...], dt_ref[...], dtb_ref[...], a_ref[...], dsk_ref[...],
                            cb, lmask, valid)
    acs_last = jnp.dot(jnp.where(last, 1.0, 0.0), acs, precision=HI, preferred_element_type=F32)
    e_acs = jnp.exp(acs)
    xd_t = (xdt * jnp.exp(acs_last - acs)).T
    cd_t = jnp.exp(acs_last).T
    lane = lax.broadcasted_iota(I32, (LANES, ll), 1)
    cm32 = cm_ref[...]
    y_off = []
    for s in range(n_seq):
        rs = slice(s * SEQ_PAD, (s + 1) * SEQ_PAD)
        state = h0_ref[s].reshape(LANES, D_STATE)
        y_off.append(lax.dot_general(cm32[rs].astype(BF16), state.astype(BF16), NT_DIMS,
                                     preferred_element_type=F32) * e_acs[rs])
        in_seq = (lane >= s * SEQ_PAD) & (lane < (s + 1) * SEQ_PAD)
        upd = jnp.dot(jnp.where(in_seq, xd_t, 0.0).astype(BF16), bm, preferred_element_type=F32)
        new = state * cd_t[:, s * SEQ_PAD:s * SEQ_PAD + 1] + upd
        h1_ref[s] = new.reshape(2, SSM_HEADDIM, D_STATE)
    y_ref[...] = y + jnp.concatenate(y_off, axis=0)


def ssd_sample(xc, dt_exp, dtb_exp, a_exp, dsk_exp, h0, d_inner, t_valid, n_seq=16):
    rows = xc.shape[0]
    b = h0.shape[0]
    n_pairs = d_inner // LANES
    ppg = n_pairs // SSM_GROUPS
    ll = n_seq * SEQ_PAD
    first_b = d_inner // D_STATE
    return pl.pallas_call(
        functools.partial(_ssd_sample_body, n_seq=n_seq, t_valid=t_valid),
        grid=(b // n_seq, n_pairs),
        in_specs=[
            pl.BlockSpec((ll, LANES), lambda s, k: (s, k)),
            pl.BlockSpec((ll, D_STATE), lambda s, k: (s, first_b + k // ppg)),
            pl.BlockSpec((ll, D_STATE), lambda s, k: (s, first_b + SSM_GROUPS + k // ppg)),
            pl.BlockSpec((ll, LANES), lambda s, k: (s, k)),
            pl.BlockSpec((1, LANES), lambda s, k: (0, k)),
            pl.BlockSpec((1, LANES), lambda s, k: (0, k)),
            pl.BlockSpec((1, LANES), lambda s, k: (0, k)),
            pl.BlockSpec((n_seq, 2, SSM_HEADDIM, D_STATE), lambda s, k: (s, k, 0, 0)),
        ],
        out_specs=[
            pl.BlockSpec((ll, LANES), lambda s, k: (s, k)),
            pl.BlockSpec((n_seq, 2, SSM_HEADDIM, D_STATE), lambda s, k: (s, k, 0, 0)),
        ],
        out_shape=[
            jax.ShapeDtypeStruct((rows, d_inner), F32),
            jax.ShapeDtypeStruct(h0.shape, F32),
        ],
        compiler_params=_cparams(("arbitrary", "arbitrary"), 32),
        name="ssd_sample",
    )(xc, xc, xc, dt_exp, dtb_exp, a_exp, dsk_exp, h0)


def _gated_norm_body(y_ref, z_ref, g_ref, o_ref, *, d_inner):
    z = z_ref[...]
    yg = y_ref[...] * (z * _sigmoid(z))
    gw = d_inner // SSM_GROUPS
    for g in range(SSM_GROUPS):
        gs = slice(g * gw, (g + 1) * gw)
        v = yg[:, gs]
        r = lax.rsqrt(jnp.mean(v * v, axis=-1, keepdims=True) + EPS)
        o_ref[:, gs] = (v * r * g_ref[:, gs]).astype(BF16)


def gated_norm(y, z, norm_g, d_inner, tm=TM):
    t = y.shape[0]
    return pl.pallas_call(
        functools.partial(_gated_norm_body, d_inner=d_inner),
        grid=(t // tm,),
        in_specs=[
            pl.BlockSpec((tm, d_inner), lambda i: (i, 0)),
            pl.BlockSpec((tm, d_inner), lambda i: (i, 0)),
            _const_spec((1, d_inner)),
        ],
        out_specs=pl.BlockSpec((tm, d_inner), lambda i: (i, 0)),
        out_shape=jax.ShapeDtypeStruct((t, d_inner), BF16),
        compiler_params=_cparams(("arbitrary",), 32),
        name="gated_norm",
    )(y, z, norm_g.reshape(1, d_inner))


def _pack_cols(w, pieces, total):
    cols = [w[:, a:b] for a, b in pieces]
    used = sum(b - a for a, b in pieces)
    if total > used:
        cols.append(jnp.zeros((w.shape[0], total - used), w.dtype))
    return jnp.concatenate(cols, axis=1).astype(BF16)


def _pad_rows(a, rows, axis):
    pad = [(0, 0)] * a.ndim
    pad[axis] = (0, rows - a.shape[axis])
    return jnp.pad(a, pad)


def _dsa_layer(x_all, n_prompt, n_seq, t_s, mix_g, w_in, rel_bias, cache_k, cache_v, cache_kidx, layer,
               page_table):
    qw, kvw, qiw = N_HEADS_A * HEAD_DIM, KV_HEADS * HEAD_DIM, IDX_HEADS * IDX_DIM
    o_q, o_k, o_v, o_qi = 0, qw, qw + kvw, qw + 2 * kvw
    o_ki = o_qi + qiw
    o_wi = o_ki + IDX_DIM
    o_qm = o_wi + IDX_HEADS
    col = {"q": 0, "k": qw, "qi": qw + kvw, "v": qw + kvw + qiw, "qm": qw + 2 * kvw + qiw}
    col["kw"] = col["qm"] + MEM_WIDTH
    width = col["kw"] + LANES
    w = _pack_cols(w_in, [(o_q, o_q + qw), (o_k, o_k + kvw), (o_qi, o_qi + qiw), (o_v, o_v + kvw),
                          (o_qm, o_qm + MEM_WIDTH), (o_ki, o_ki + IDX_DIM), (o_wi, o_wi + IDX_HEADS)], width)
    z = norm_matmul(x_all, mix_g, w, TM, width // 3)

    k_all = z[:, col["k"]:col["k"] + kvw]
    v_all = z[:, col["v"]:col["v"] + kvw]
    ki_all = z[:, col["kw"]:col["kw"] + IDX_DIM]

    kiw_bf = z[:n_prompt, col["kw"]:col["kw"] + LANES].astype(BF16)
    mix_p = dsa_prompt(z, n_prompt, col, kiw_bf, k_all[:n_prompt].astype(BF16),
                       v_all[:n_prompt].astype(BF16), rel_bias)

    zs = z[n_prompt:].reshape(n_seq, t_s, width)
    zs8 = _pad_rows(zs, SEQ_PAD, 1)
    qi8 = zs8[:, :, col["qi"]:col["qi"] + qiw].reshape(n_seq, SEQ_PAD * IDX_HEADS, IDX_DIM)
    w8 = zs8[:, :, col["kw"] + IDX_DIM:col["kw"] + IDX_DIM + IDX_HEADS].reshape(n_seq, SEQ_PAD * IDX_HEADS, 1)
    q8 = zs8[:, :, :qw].reshape(n_seq, SEQ_PAD, N_HEADS_A, HEAD_DIM).transpose(0, 2, 1, 3)
    q8 = q8.reshape(n_seq, N_HEADS_A * SEQ_PAD, HEAD_DIM)
    kidx_new = _pad_rows(zs[:, :, col["kw"]:col["kw"] + IDX_DIM], PAGE_SIZE, 1)
    k_new = _pad_rows(zs[:, :, col["k"]:col["k"] + kvw], PAGE_SIZE, 1)
    v_new = _pad_rows(zs[:, :, col["v"]:col["v"] + kvw], PAGE_SIZE, 1)
    o_s = dsa_sample(qi8, w8, q8, kidx_new, k_new, v_new, cache_kidx, cache_k, cache_v, layer,
                     page_table, rel_bias, t_s)
    mix_s = o_s.reshape(n_seq, N_HEADS_A, SEQ_PAD, HEAD_DIM)[:, :, :t_s].transpose(0, 2, 1, 3)
    mix_s = mix_s.reshape(n_seq * t_s, qw).astype(BF16)
    return z, col["qm"] // MEM_WIDTH, jnp.concatenate([mix_p, mix_s], axis=0), k_all, v_all, ki_all


def _ssd_layer(x_all, n_prompt, n_seq, t_s, mix_g, w_in, conv_w, conv_b, dt_bias, a_log, d_skip, norm_g,
               state_conv, state_ssm):
    n_heads = dt_bias.shape[0]
    d_inner = n_heads * SSM_HEADDIM
    cdim = d_inner + 2 * SSM_GROUPS * D_STATE
    o_x, o_dt = d_inner, d_inner + cdim
    o_qm = o_dt + n_heads
    c_qm = d_inner + cdim
    c_dt = c_qm + MEM_WIDTH
    width = _round_up(c_dt + n_heads, 2 * LANES)
    w = _pack_cols(w_in, [(0, d_inner), (o_x, o_x + cdim), (o_qm, o_qm + MEM_WIDTH), (o_dt, o_dt + n_heads)], width)
    z = norm_matmul(x_all, mix_g, w, TM, width // 2)
    xbc = z[:, d_inner:d_inner + cdim]
    dt_exp = jnp.repeat(z[:, c_dt:c_dt + n_heads], SSM_HEADDIM, axis=1)
    dtb_exp = jnp.repeat(dt_bias.astype(F32), SSM_HEADDIM).reshape(1, d_inner)
    a_exp = -jnp.exp(jnp.repeat(a_log.astype(F32), SSM_HEADDIM)).reshape(1, d_inner)
    dsk_exp = jnp.repeat(d_skip.astype(F32), SSM_HEADDIM).reshape(1, d_inner)

    xbc_p = xbc[:n_prompt][None]
    xc_p = conv_silu(xbc_p, 0, jnp.zeros((1, SUBLANES, cdim), F32), conv_w, conv_b, SSD_L, cdim)[0]
    y_p, hf_p = ssd_prompt(xc_p, dt_exp, dtb_exp, a_exp, dsk_exp, d_inner)
    new_conv_p = xbc[n_prompt - (CONV_W - 1):n_prompt]

    xbc_s = xbc[n_prompt:].reshape(n_seq, t_s, cdim)
    xbc_s8 = _pad_rows(xbc_s, SEQ_PAD, 1)
    st8 = jnp.concatenate([jnp.zeros((n_seq, SUBLANES - (CONV_W - 1), cdim), F32), state_conv.astype(F32)], axis=1)
    xc_s = conv_silu(xbc_s8, 0, st8, conv_w, conv_b, SEQ_PAD, cdim).reshape(n_seq * SEQ_PAD, cdim)
    dt_s8 = _pad_rows(dt_exp[n_prompt:].reshape(n_seq, t_s, d_inner), SEQ_PAD, 1).reshape(n_seq * SEQ_PAD, d_inner)
    y_s8, hf_s = ssd_sample(xc_s, dt_s8, dtb_exp, a_exp, dsk_exp, state_ssm.astype(F32), d_inner, t_s)
    y_s = y_s8.reshape(n_seq, SEQ_PAD, d_inner)[:, :t_s].reshape(n_seq * t_s, d_inner)
    new_conv_s = jnp.concatenate([state_conv.astype(F32), xbc_s], axis=1)[:, -(CONV_W - 1):]

    mix = gated_norm(jnp.concatenate([y_p, y_s], axis=0), z, norm_g, d_inner)
    return z, c_qm // MEM_WIDTH, mix, hf_p, new_conv_p, hf_s, new_conv_s


def kernel(x_prompt, x_sample, mem_prompt, cache_k, cache_v, cache_kidx, page_table, state_ssm, state_conv,
           cache_mem_k, cache_mem_v, rel_bias, ffn1_g, ffn1_w_gu, ffn1_w_down, mix_g, mem_g, w_mem_kv,
           w_in_attn, w_in_ssd, conv_w, conv_b, dt_bias, a_log, d_skip, ssd_norm_g, w_out,
           ffn2_g, ffn2_w_gu, ffn2_w_down, final_g):
    bp, n_prompt, d = x_prompt.shape
    n_seq, t_s, _ = x_sample.shape
    assert bp == 1
    depth = ffn1_g.shape[0]
    n_mem = mem_prompt.shape[1]
    x_all = jnp.concatenate([x_prompt[0], x_sample.reshape(n_seq * t_s, d)], axis=0)
    outs = {k: [] for k in ("pk", "pv", "pki", "pssm", "pconv", "pmk", "pmv", "sk", "sv", "ski", "sssm", "sconv")}
    y_all = None
    for i in range(depth):
        j = i // 2
        x_all = ffn(x_all, ffn1_g[i], ffn1_w_gu[i], ffn1_w_down[i])
        mkv = norm_matmul(mem_prompt[0], mem_g[i], w_mem_kv[i].astype(BF16), n_mem, 2 * MEM_WIDTH)
        outs["pmk"].append(mkv[:, :MEM_WIDTH].reshape(1, n_mem, MEM_HEADS, MEM_HEAD_DIM))
        outs["pmv"].append(mkv[:, MEM_WIDTH:].reshape(1, n_mem, MEM_HEADS, MEM_HEAD_DIM))
        if i % 2 == 0:
            z, qm_blk, mix, k_all, v_all, ki_all = _dsa_layer(
                x_all, n_prompt, n_seq, t_s, mix_g[i], w_in_attn[j], rel_bias,
                cache_k, cache_v, cache_kidx, j, page_table)
            outs["pk"].append(k_all[:n_prompt].reshape(1, n_prompt, KV_HEADS, HEAD_DIM))
            outs["pv"].append(v_all[:n_prompt].reshape(1, n_prompt, KV_HEADS, HEAD_DIM))
            outs["pki"].append(ki_all[:n_prompt].reshape(1, n_prompt, IDX_DIM))
            outs["sk"].append(k_all[n_prompt:].reshape(n_seq, t_s, KV_HEADS, HEAD_DIM))
            outs["sv"].append(v_all[n_prompt:].reshape(n_seq, t_s, KV_HEADS, HEAD_DIM))
            outs["ski"].append(ki_all[n_prompt:].reshape(n_seq, t_s, IDX_DIM))
        else:
            z, qm_blk, mix, hf_p, conv_p, hf_s, conv_s = _ssd_layer(
                x_all, n_prompt, n_seq, t_s, mix_g[i], w_in_ssd[j], conv_w[j], conv_b[j], dt_bias[j],
                a_log[j], d_skip[j], ssd_norm_g[j], state_conv[j], state_ssm[j])
            outs["pssm"].append(hf_p[None])
            outs["pconv"].append(conv_p[None])
            outs["sssm"].append(hf_s)
            outs["sconv"].append(conv_s)
        mem_p = mem_attn_prompt(z, qm_blk, n_prompt, mkv)
        qm_s = z[n_prompt:, qm_blk * MEM_WIDTH:(qm_blk + 1) * MEM_WIDTH].reshape(n_seq, t_s, MEM_WIDTH)
        mem_s = mem_attn_sample(_pad_rows(qm_s, SEQ_PAD, 1), cache_mem_k, cache_mem_v, i)
        mem = jnp.concatenate([mem_p, mem_s[:, :t_s].reshape(n_seq * t_s, MEM_WIDTH)], axis=0)
        x_all = out_proj(x_all, mix, mem, w_out[i])
        if i == depth - 1:
            x_all, y_all = ffn(x_all, ffn2_g[i], ffn2_w_gu[i], ffn2_w_down[i], final_g=final_g)
        else:
            x_all = ffn(x_all, ffn2_g[i], ffn2_w_gu[i], ffn2_w_down[i])
    y_prompt = y_all[:n_prompt][None]
    y_sample = y_all[n_prompt:].reshape(n_seq, t_s, d)
    st = lambda k: jnp.stack(outs[k])
    return (y_prompt, y_sample, st("pk"), st("pv"), st("pki"), st("pssm"), st("pconv"), st("pmk"), st("pmv"),
            st("sk"), st("sv"), st("ski"), st("sssm"), st("sconv"))
```

```python
import functools
import math

import numpy as np
import jax
import jax.numpy as jnp
from jax import lax
from jax.experimental import pallas as pl
from jax.experimental.pallas import tpu as pltpu

F32, BF16, I32 = jnp.float32, jnp.bfloat16, jnp.int32
HI = lax.Precision.HIGHEST
NT_DIMS = (((1,), (1,)), ((), ()))

HEAD_DIM = 128
KV_HEADS = 4
Q_PER_KV = 3
N_HEADS_A = KV_HEADS * Q_PER_KV
IDX_HEADS = 16
IDX_DIM = 64
TOPK_MAX = 256
REL_BUCKETS = 32
REL_MAX_EXACT = 16
REL_MAX_DIST = 128
MEM_HEADS = 4
MEM_HEAD_DIM = 128
MEM_WIDTH = MEM_HEADS * MEM_HEAD_DIM
SSM_HEADDIM = 64
SSM_GROUPS = 4
D_STATE = 128
CONV_W = 4
PAGE_SIZE = 128
EPS = 1e-6

LANES = 128
SUBLANES = 8
NEG = -1e30
INT_MIN = -2 ** 31
IDX_SCALE = IDX_DIM ** -0.5 * IDX_HEADS ** -0.5
ATT_SCALE = HEAD_DIM ** -0.5
MEM_SCALE = MEM_HEAD_DIM ** -0.5

QB = 128
KC = 512
TM = 512
TF = 512
SSD_L = 128
SEQ_PAD = 8


def _cparams(sem, vmem_mb):
    return pltpu.CompilerParams(dimension_semantics=sem, vmem_limit_bytes=vmem_mb * 2 ** 20)


def _round_up(n, m):
    return (n + m - 1) // m * m


def _const_spec(shape):
    nd = len(shape)
    return pl.BlockSpec(shape, lambda *_: (0,) * nd)


def _resident_spec(shape):
    nd = len(shape)
    return pl.BlockSpec(shape, lambda *_: (0,) * nd, pipeline_mode=pl.Buffered(1))


def _rms(x, g):
    return x * lax.rsqrt(jnp.mean(x * x, axis=-1, keepdims=True) + EPS) * g


def _sigmoid(x):
    return 1.0 / (1.0 + jnp.exp(-x))


def _softplus(x):
    return jnp.maximum(x, 0.0) + jnp.log1p(jnp.exp(-jnp.abs(x)))


def _ffn_body(x_ref, g_ref, wg_ref, wu_ref, wd_ref, *rest, nj, with_final):
    if with_final:
        fg_ref, o_ref, y_ref, xn_ref, acc_ref = rest
    else:
        o_ref, xn_ref, acc_ref = rest
    j = pl.program_id(1)

    @pl.when(j == 0)
    def _():
        xn_ref[...] = _rms(x_ref[...], g_ref[...]).astype(BF16)
        acc_ref[...] = jnp.zeros_like(acc_ref)

    xn = xn_ref[...]
    gate = jnp.dot(xn, wg_ref[...], preferred_element_type=F32)
    up = jnp.dot(xn, wu_ref[...], preferred_element_type=F32)
    a = gate * _sigmoid(gate) * up
    acc_ref[...] += jnp.dot(a.astype(BF16), wd_ref[...], preferred_element_type=F32)

    @pl.when(j == nj - 1)
    def _():
        o = x_ref[...] + 0.5 * acc_ref[...]
        o_ref[...] = o
        if with_final:
            y_ref[...] = _rms(o, fg_ref[...])


def _cast_gate_up_body(g_ref, u_ref, og_ref, ou_ref, *, n_valid):
    keep = pl.program_id(1) < n_valid
    og_ref[...] = jnp.where(keep, g_ref[...], 0.0).astype(BF16)
    ou_ref[...] = jnp.where(keep, u_ref[...], 0.0).astype(BF16)


def _cast_down_body(w_ref, o_ref, *, n_valid):
    o_ref[...] = jnp.where(pl.program_id(1) < n_valid, w_ref[...], 0.0).astype(BF16)


def prep_ffn_weights(w_gu, w_d, tf):
    nl, d, two_ff = w_gu.shape
    ff = two_ff // 2
    ffp = _round_up(ff, tf)
    assert ff % LANES == 0
    nb, nbp = ff // LANES, ffp // LANES
    wg, wu = pl.pallas_call(
        functools.partial(_cast_gate_up_body, n_valid=nb),
        grid=(nl, nbp),
        in_specs=[pl.BlockSpec((None, d, LANES), lambda l, j: (l, 0, jnp.minimum(j, nb - 1))),
                  pl.BlockSpec((None, d, LANES), lambda l, j: (l, 0, nb + jnp.minimum(j, nb - 1)))],
        out_specs=[pl.BlockSpec((None, d, LANES), lambda l, j: (l, 0, j))] * 2,
        out_shape=[jax.ShapeDtypeStruct((nl, d, ffp), BF16)] * 2,
        compiler_params=_cparams(("arbitrary", "arbitrary"), 32),
        name="cast_gate_up",
    )(w_gu, w_gu)
    wd = pl.pallas_call(
        functools.partial(_cast_down_body, n_valid=nb),
        grid=(nl, nbp),
        in_specs=[pl.BlockSpec((None, LANES, d), lambda l, j: (l, jnp.minimum(j, nb - 1), 0))],
        out_specs=pl.BlockSpec((None, LANES, d), lambda l, j: (l, j, 0)),
        out_shape=jax.ShapeDtypeStruct((nl, ffp, d), BF16),
        compiler_params=_cparams(("arbitrary", "arbitrary"), 32),
        name="cast_down",
    )(w_d)
    return wg, wu, wd


def ffn(x, g, weights, layer, final_g=None, tm=TM, tf=TF):
    t, d = x.shape
    wg, wu, wd = weights
    nj = wg.shape[2] // tf
    with_final = final_g is not None
    in_specs = [
        pl.BlockSpec((tm, d), lambda i, j: (i, 0)),
        _const_spec((1, d)),
        pl.BlockSpec((None, d, tf), lambda i, j: (layer, 0, j)),
        pl.BlockSpec((None, d, tf), lambda i, j: (layer, 0, j)),
        pl.BlockSpec((None, tf, d), lambda i, j: (layer, j, 0)),
    ]
    args = [x, g.reshape(1, d), wg, wu, wd]
    out_shape = [jax.ShapeDtypeStruct((t, d), F32)]
    out_specs = [pl.BlockSpec((tm, d), lambda i, j: (i, 0))]
    if with_final:
        in_specs.append(_const_spec((1, d)))
        args.append(final_g.reshape(1, d))
        out_shape.append(jax.ShapeDtypeStruct((t, d), F32))
        out_specs.append(pl.BlockSpec((tm, d), lambda i, j: (i, 0)))
    res = pl.pallas_call(
        functools.partial(_ffn_body, nj=nj, with_final=with_final),
        grid=(t // tm, nj),
        in_specs=in_specs,
        out_specs=out_specs,
        out_shape=out_shape,
        scratch_shapes=[pltpu.VMEM((tm, d), BF16), pltpu.VMEM((tm, d), F32)],
        compiler_params=_cparams(("arbitrary", "arbitrary"), 56),
        name="ffn",
    )(*args)
    return res if with_final else res[0]


def _norm_matmul_body(x_ref, g_ref, w_ref, o_ref, xn_ref):
    @pl.when(pl.program_id(1) == 0)
    def _():
        xn_ref[...] = _rms(x_ref[...], g_ref[...]).astype(BF16)

    o_ref[...] = jnp.dot(xn_ref[...], w_ref[...], preferred_element_type=F32)


def norm_matmul(x, g, w_bf16, tm, tn):
    t, d = x.shape
    n = w_bf16.shape[1]
    return pl.pallas_call(
        _norm_matmul_body,
        grid=(t // tm, n // tn),
        in_specs=[
            pl.BlockSpec((tm, d), lambda i, j: (i, 0)),
            _const_spec((1, d)),
            pl.BlockSpec((d, tn), lambda i, j: (0, j)),
        ],
        out_specs=pl.BlockSpec((tm, tn), lambda i, j: (i, j)),
        out_shape=jax.ShapeDtypeStruct((t, n), F32),
        scratch_shapes=[pltpu.VMEM((tm, d), BF16)],
        compiler_params=_cparams(("arbitrary", "arbitrary"), 48),
        name="norm_matmul",
    )(x, g.reshape(1, d), w_bf16)


def _out_proj_body(x_ref, mix_ref, mem_ref, w1_ref, w2_ref, o_ref):
    o_ref[...] = (x_ref[...]
                  + jnp.dot(mix_ref[...], w1_ref[...], preferred_element_type=F32)
                  + jnp.dot(mem_ref[...], w2_ref[...], preferred_element_type=F32))


def out_proj(x, mix, mem, w_out, tm=TM):
    t, d = x.shape
    dm, dw = mix.shape[1], mem.shape[1]
    w1 = w_out[:dm].astype(BF16)
    w2 = w_out[dm:].astype(BF16)
    return pl.pallas_call(
        _out_proj_body,
        grid=(t // tm,),
        in_specs=[
            pl.BlockSpec((tm, d), lambda i: (i, 0)),
            pl.BlockSpec((tm, dm), lambda i: (i, 0)),
            pl.BlockSpec((tm, dw), lambda i: (i, 0)),
            _const_spec((dm, d)),
            _const_spec((dw, d)),
        ],
        out_specs=pl.BlockSpec((tm, d), lambda i: (i, 0)),
        out_shape=jax.ShapeDtypeStruct((t, d), F32),
        compiler_params=_cparams(("arbitrary",), 48),
        name="out_proj",
    )(x, mix, mem, w1, w2)


def _softmax_rows(s):
    m = jnp.max(s, axis=-1, keepdims=True)
    p = jnp.exp(s - m)
    return p, jnp.sum(p, axis=-1, keepdims=True)


def _mem_attn_prompt_body(q_ref, mk_ref, mv_ref, o_ref):
    for h in range(MEM_HEADS):
        sl = slice(h * MEM_HEAD_DIM, (h + 1) * MEM_HEAD_DIM)
        q = (q_ref[:, sl] * MEM_SCALE).astype(BF16)
        s = lax.dot_general(q, mk_ref[:, sl].astype(BF16), NT_DIMS, preferred_element_type=F32)
        p, l = _softmax_rows(s)
        o = jnp.dot(p.astype(BF16), mv_ref[:, sl].astype(BF16), preferred_element_type=F32)
        o_ref[:, sl] = (o / l).astype(BF16)


def mem_attn_prompt(z, qm_col_block, n_rows, mkv, tm=TM):
    m = mkv.shape[0]
    return pl.pallas_call(
        _mem_attn_prompt_body,
        grid=(n_rows // tm,),
        in_specs=[
            pl.BlockSpec((tm, MEM_WIDTH), lambda i: (i, qm_col_block)),
            pl.BlockSpec((m, MEM_WIDTH), lambda i: (0, 0)),
            pl.BlockSpec((m, MEM_WIDTH), lambda i: (0, 1)),
        ],
        out_specs=pl.BlockSpec((tm, MEM_WIDTH), lambda i: (i, 0)),
        out_shape=jax.ShapeDtypeStruct((n_rows, MEM_WIDTH), BF16),
        compiler_params=_cparams(("arbitrary",), 32),
        name="mem_attn_prompt",
    )(z, mkv, mkv)


def _mem_attn_sample_body(q_ref, k_ref, v_ref, o_ref, *, bb, m):
    rows, cols = MEM_HEADS * SEQ_PAD, m * MEM_HEADS
    row_head = lax.shift_right_logical(lax.broadcasted_iota(I32, (rows, cols), 0), SEQ_PAD.bit_length() - 1)
    col_head = lax.broadcasted_iota(I32, (rows, cols), 1) & (MEM_HEADS - 1)
    madd = jnp.where(row_head == col_head, 0.0, NEG)
    for b in range(bb):
        q = jnp.concatenate([q_ref[b, :, h * MEM_HEAD_DIM:(h + 1) * MEM_HEAD_DIM] for h in range(MEM_HEADS)], axis=0)
        s = lax.dot_general((q * MEM_SCALE).astype(BF16), k_ref[b].astype(BF16), NT_DIMS,
                            preferred_element_type=F32)
        p, l = _softmax_rows(s + madd)
        o = jnp.dot(p.astype(BF16), v_ref[b].astype(BF16), preferred_element_type=F32) / l
        for h in range(MEM_HEADS):
            o_ref[b, :, h * MEM_HEAD_DIM:(h + 1) * MEM_HEAD_DIM] = o[h * SEQ_PAD:(h + 1) * SEQ_PAD].astype(BF16)


def mem_attn_sample(qm8, mem_k, mem_v, layer, bb=8):
    depth, b, m, nh, hd = mem_k.shape
    mem_k = mem_k.reshape(depth, b, m * nh, hd)
    mem_v = mem_v.reshape(depth, b, m * nh, hd)
    cache_spec = pl.BlockSpec((None, bb, m * nh, hd), lambda i: (layer, i, 0, 0))
    return pl.pallas_call(
        functools.partial(_mem_attn_sample_body, bb=bb, m=m),
        grid=(b // bb,),
        in_specs=[pl.BlockSpec((bb, SEQ_PAD, MEM_WIDTH), lambda i: (i, 0, 0)), cache_spec, cache_spec],
        out_specs=pl.BlockSpec((bb, SEQ_PAD, MEM_WIDTH), lambda i: (i, 0, 0)),
        out_shape=jax.ShapeDtypeStruct((b, SEQ_PAD, MEM_WIDTH), BF16),
        compiler_params=_cparams(("arbitrary",), 40),
        name="mem_attn_sample",
    )(qm8, mem_k, mem_v)


def _sortable_key(x):
    b = pltpu.bitcast(x, I32)
    return jnp.where(b < 0, (b ^ 0x7FFFFFFF) + 1, b)


def _t5_bucket_np(dist):
    n = np.maximum(dist, 0)
    nf = np.maximum(n, 1).astype(np.float64)
    large = REL_MAX_EXACT + (np.log(nf / REL_MAX_EXACT) / math.log(REL_MAX_DIST / REL_MAX_EXACT)
                             * (REL_BUCKETS - REL_MAX_EXACT)).astype(np.int32)
    large = np.minimum(large, REL_BUCKETS - 1)
    return np.where(n < REL_MAX_EXACT, n, large)


FAR_DIST = int(np.min(np.nonzero(_t5_bucket_np(np.arange(4 * REL_MAX_DIST)) == REL_BUCKETS - 1)[0]))
assert np.all(_t5_bucket_np(np.arange(FAR_DIST, 1 << 16)) == REL_BUCKETS - 1) and FAR_DIST <= QB


def _toeplitz_bias(rel_bias, n, m, k0):
    p = n + m
    d = k0 + (n - 1) - np.arange(p)
    b = _t5_bucket_np(d)
    keep = (d >= 0) & (b != REL_BUCKETS - 1)
    u = jnp.where(keep[:, None], rel_bias[b] - rel_bias[REL_BUCKETS - 1], 0.0).T
    skew = jnp.tile(u, (1, n))[:, :n * (p - 1)].reshape(-1, n, p - 1)
    return skew[:, :, n - 1:n - 1 + m].astype(F32)


def _threshold_search(count_ge, count_tie_lt, shape, topk, n_idx_bits, all_idx):
    zero = jnp.zeros(shape, I32)
    c0 = count_ge(zero)
    t = jnp.where(c0 >= topk, zero, jnp.full(shape, INT_MIN, I32))
    n_ge = jnp.where(c0 >= topk, c0, -NEG)

    def bit_body(b, carry):
        t, n_ge = carry
        cand = t | lax.shift_left(jnp.int32(1), 30 - b)
        c = count_ge(cand)
        ok = c >= topk
        return jnp.where(ok, cand, t), jnp.where(ok, c, n_ge)

    t, n_ge = lax.fori_loop(0, 31, bit_body, (t, n_ge))
    has_k = t > INT_MIN
    excess = jnp.max(jnp.where(has_k, n_ge, 0.0)) > topk

    def tie_search(_):
        need = topk - count_ge(t + 1)

        def jbit(b, j):
            cand = j | lax.shift_left(jnp.int32(1), n_idx_bits - 1 - b)
            return jnp.where(count_tie_lt(t, cand) < need, cand, j)
        return lax.fori_loop(0, n_idx_bits, jbit, jnp.zeros(shape, I32))

    j = lax.cond(excess, tie_search, lambda _: jnp.full(shape, all_idx, I32), 0)
    j = jnp.where(has_k, j, -1)
    return t, j


def _dsa_prompt_body(q_ref, qi_ref, kw_ref, kiw_ref, k_ref, v_ref, bias_ref, o_ref,
                     rhs_ref, wt_ref, key_ref, qe_ref, m_ref, acc_ref, p_ref, s_ref, *, seq, topk):
    i = pl.program_id(0)
    q0 = i * QB

    qi_t = qi_ref[...].T
    pad = jnp.zeros((LANES - IDX_DIM, QB), F32)
    for h in range(IDX_HEADS):
        blk = jnp.concatenate([qi_t[h * IDX_DIM:(h + 1) * IDX_DIM], pad], axis=0)
        rhs_ref[:, h * QB:(h + 1) * QB] = blk.astype(BF16)
    wt_ref[...] = kw_ref[...].T * IDX_SCALE

    n_chunks = (jnp.maximum(i + 1, 2) * QB + KC - 1) // KC

    def score_chunk(c, carry):
        c0 = pl.multiple_of(c * KC, KC)
        x = jnp.dot(kiw_ref[pl.ds(c0, KC), :], rhs_ref[...], preferred_element_type=F32)
        acc = jnp.zeros((KC, QB), F32)
        for h in range(IDX_HEADS):
            acc = acc + jnp.maximum(x[:, h * QB:(h + 1) * QB], 0.0) * wt_ref[IDX_DIM + h:IDX_DIM + h + 1, :]
        kpos = c0 + lax.broadcasted_iota(I32, (KC, QB), 0)
        qpos = q0 + lax.broadcasted_iota(I32, (KC, QB), 1)
        key_ref[pl.ds(c0, KC), :] = jnp.where(kpos <= qpos, _sortable_key(acc), INT_MIN)
        return carry

    lax.fori_loop(0, n_chunks, score_chunk, 0)

    def column_count(hit_of_chunk):
        acc_rows = 8 * SUBLANES

        def body(c, a):
            c0 = pl.multiple_of(c * KC, KC)
            hit = jnp.where(hit_of_chunk(key_ref[pl.ds(c0, KC), :], c0), 1.0, 0.0)
            return a + hit.reshape(KC // acc_rows, acc_rows, QB).sum(axis=0)

        a = lax.fori_loop(0, n_chunks, body, jnp.zeros((acc_rows, QB), F32))
        return jnp.sum(a, axis=0, keepdims=True)

    def count_ge(cand):
        return column_count(lambda kk, c0: kk >= cand)

    def count_tie_lt(t, jc):
        row = lax.broadcasted_iota(I32, (KC, QB), 0)
        return column_count(lambda kk, c0: (kk == t) & (c0 + row < jc))

    t, j = _threshold_search(count_ge, count_tie_lt, (1, QB), topk, (seq - 1).bit_length(), seq)

    eye = jnp.where(lax.broadcasted_iota(I32, (QB, QB), 0) == lax.broadcasted_iota(I32, (QB, QB), 1),
                    1.0, 0.0).astype(BF16)
    for g in range(KV_HEADS):
        for r in range(Q_PER_KV):
            hq = g * Q_PER_KV + r
            rs = slice(r * QB, (r + 1) * QB)
            qe_ref[g, rs, 0:HEAD_DIM] = (q_ref[:, hq * HEAD_DIM:(hq + 1) * HEAD_DIM] * ATT_SCALE).astype(BF16)
            qe_ref[g, rs, HEAD_DIM:2 * HEAD_DIM] = eye
    m_ref[...] = jnp.full(m_ref.shape, 3 * NEG, F32)
    acc_ref[...] = jnp.zeros(acc_ref.shape, F32)

    def mask_t(c0, width, far_end):
        kk = key_ref[pl.ds(c0, width), :]
        kpos = c0 + lax.broadcasted_iota(I32, (width, QB), 0)
        sel = (kk > t) | ((kk == t) & (kpos <= j))
        if far_end is not None:
            sel = sel & (kpos < far_end)
        return jnp.where(sel, 0.0, NEG).astype(BF16)

    def add_pv(g, p, c0, width):
        gs = slice(g * HEAD_DIM, (g + 1) * HEAD_DIM)
        v_ext = jnp.concatenate([v_ref[pl.ds(c0, width), gs], jnp.ones((width, HEAD_DIM), BF16)], axis=1)
        acc_ref[g] = acc_ref[g] + jnp.dot(p, v_ext, preferred_element_type=F32)

    def logits(g, c0, width, madd_t):
        gs = slice(g * HEAD_DIM, (g + 1) * HEAD_DIM)
        k_ext = jnp.concatenate([k_ref[pl.ds(c0, width), gs], madd_t], axis=1)
        return lax.dot_general(qe_ref[g], k_ext, NT_DIMS, preferred_element_type=F32)

    def probs(g, s):
        m_old = m_ref[g]
        m_new = jnp.maximum(m_old, jnp.max(s, axis=1, keepdims=True))
        alpha = jnp.exp(m_old - m_new)
        acc_ref[g] = jnp.concatenate([alpha, alpha], axis=1) * acc_ref[g]
        m_ref[g] = m_new
        return jnp.exp((s - m_new[:, 0:1]).astype(BF16))

    far_end = jnp.maximum(i - 1, 0) * QB
    n_far = (far_end + KC - 1) // KC
    last0 = pl.multiple_of(jnp.maximum(n_far - 1, 0) * KC, KC)
    p_ref[...] = jnp.zeros(p_ref.shape, BF16)
    madd0 = mask_t(0, KC, far_end)
    for g in range(KV_HEADS):
        s_ref[g] = logits(g, 0, KC, madd0)

    def far_body(c, carry):
        prev0 = pl.multiple_of(jnp.maximum(c - 1, 0) * KC, KC)
        next0 = pl.multiple_of(jnp.minimum((c + 1) * KC, last0), KC)
        madd_next = mask_t(next0, KC, far_end)
        for g in range(KV_HEADS):
            add_pv(g, p_ref[g], prev0, KC)
            s_next = logits(g, next0, KC, madd_next)
            p_ref[g] = probs(g, s_ref[g])
            s_ref[g] = s_next
        return carry

    lax.fori_loop(0, n_far, far_body, 0)
    near0 = pl.multiple_of(far_end, QB)
    variant = jnp.minimum(i, 1)
    madd_near = mask_t(near0, 2 * QB, None)
    for g in range(KV_HEADS):
        add_pv(g, p_ref[g], last0, KC)
        s = logits(g, near0, 2 * QB, madd_near) + bias_ref[variant, g]
        add_pv(g, probs(g, s), near0, 2 * QB)

    for g in range(KV_HEADS):
        o = acc_ref[g, :, 0:HEAD_DIM] / acc_ref[g, :, HEAD_DIM:2 * HEAD_DIM]
        for r in range(Q_PER_KV):
            hq = g * Q_PER_KV + r
            o_ref[:, hq * HEAD_DIM:(hq + 1) * HEAD_DIM] = o[r * QB:(r + 1) * QB].astype(BF16)


def _prompt_bias(rel_bias):
    out = [_toeplitz_bias(rel_bias, QB, 2 * QB, k0).reshape(KV_HEADS, Q_PER_KV * QB, 2 * QB) for k0 in (0, QB)]
    return jnp.stack(out)


def dsa_prompt(z, seq, col, kiw_bf, k_bf, v_bf, rel_bias):
    topk = min(TOPK_MAX, seq // 4)
    assert seq % KC == 0 and seq >= 2 * QB
    qw = N_HEADS_A * HEAD_DIM
    qiw = IDX_HEADS * IDX_DIM
    bias = _prompt_bias(rel_bias)
    rows = Q_PER_KV * QB
    return pl.pallas_call(
        functools.partial(_dsa_prompt_body, seq=seq, topk=topk),
        grid=(seq // QB,),
        in_specs=[
            pl.BlockSpec((QB, qw), lambda i: (i, col["q"] // qw)),
            pl.BlockSpec((QB, qiw), lambda i: (i, col["qi"] // qiw)),
            pl.BlockSpec((QB, LANES), lambda i: (i, col["kw"] // LANES)),
            _resident_spec((seq, LANES)),
            _resident_spec((seq, KV_HEADS * HEAD_DIM)),
            _resident_spec((seq, KV_HEADS * HEAD_DIM)),
            _resident_spec(bias.shape),
        ],
        out_specs=pl.BlockSpec((QB, qw), lambda i: (i, 0)),
        out_shape=jax.ShapeDtypeStruct((seq, qw), BF16),
        scratch_shapes=[
            pltpu.VMEM((LANES, IDX_HEADS * QB), BF16),
            pltpu.VMEM((LANES, QB), F32),
            pltpu.VMEM((seq, QB), I32),
            pltpu.VMEM((KV_HEADS, rows, 2 * HEAD_DIM), BF16),
            pltpu.VMEM((KV_HEADS, rows, LANES), F32),
            pltpu.VMEM((KV_HEADS, rows, 2 * HEAD_DIM), F32),
            pltpu.VMEM((KV_HEADS, rows, KC), BF16),
            pltpu.VMEM((KV_HEADS, rows, KC), F32),
        ],
        compiler_params=_cparams(("arbitrary",), 56),
        name="dsa_prompt",
    )(z, z, z, kiw_bf, k_bf, v_bf, bias)


def _dsa_sample_body(pt_ref, qi_ref, w_ref, q_ref, bias_ref, *refs, n_pages, t_valid, topk):
    del pt_ref
    np1 = n_pages + 1
    kidx_refs, k_refs, v_refs = refs[0:np1], refs[np1:2 * np1], refs[2 * np1:3 * np1]
    o_ref, key_ref, s_ref = refs[3 * np1:]
    past = n_pages * PAGE_SIZE
    width = np1 * PAGE_SIZE
    rows = KV_HEADS * Q_PER_KV * SEQ_PAD

    qi = qi_ref[...].astype(BF16)
    n_th = SEQ_PAD * IDX_HEADS
    same_t = (lax.shift_right_logical(lax.broadcasted_iota(I32, (SEQ_PAD, n_th), 1), IDX_HEADS.bit_length() - 1)
              == lax.broadcasted_iota(I32, (SEQ_PAD, n_th), 0))
    w_sel = jnp.where(same_t, w_ref[...] * IDX_SCALE, 0.0)
    for p in range(np1):
        x = jnp.dot(qi, kidx_refs[p][...].astype(BF16), preferred_element_type=F32)
        sc = jnp.dot(w_sel, jnp.maximum(x, 0.0), precision=HI, preferred_element_type=F32)
        kpos = p * PAGE_SIZE + lax.broadcasted_iota(I32, (SEQ_PAD, PAGE_SIZE), 1)
        qpos = past + lax.broadcasted_iota(I32, (SEQ_PAD, PAGE_SIZE), 0)
        ok = (kpos <= qpos) & (kpos < past + t_valid)
        key_ref[:, p * PAGE_SIZE:(p + 1) * PAGE_SIZE] = jnp.where(ok, _sortable_key(sc), INT_MIN)

    keys = key_ref[...]
    kpos = lax.broadcasted_iota(I32, (SEQ_PAD, width), 1)

    def count_ge(cand):
        return jnp.sum(jnp.where(keys >= cand, 1.0, 0.0), axis=1, keepdims=True)

    def count_tie_lt(t, jc):
        return jnp.sum(jnp.where((keys == t) & (kpos < jc), 1.0, 0.0), axis=1, keepdims=True)

    t, j = _threshold_search(count_ge, count_tie_lt, (SEQ_PAD, 1), topk, (width - 1).bit_length(), width)
    sel = (keys > t) | ((keys == t) & (kpos <= j))
    madd = jnp.where(sel, 0.0, NEG)
    madd = jnp.concatenate([madd] * (KV_HEADS * Q_PER_KV), axis=0)

    def kv_tile(page_refs, p, g):
        if p < n_pages:
            return page_refs[p][pl.ds(g, PAGE_SIZE, stride=KV_HEADS), :].astype(BF16)
        return page_refs[p][:, g * HEAD_DIM:(g + 1) * HEAD_DIM].astype(BF16)

    grp_rows = Q_PER_KV * SEQ_PAD
    for g in range(KV_HEADS):
        rs = slice(g * grp_rows, (g + 1) * grp_rows)
        qg = (q_ref[rs, :] * ATT_SCALE).astype(BF16)
        for p in range(np1):
            s_ref[rs, p * PAGE_SIZE:(p + 1) * PAGE_SIZE] = lax.dot_general(
                qg, kv_tile(k_refs, p, g), NT_DIMS, preferred_element_type=F32)
    pr, l = _softmax_rows(s_ref[...] + bias_ref[...] + madd)
    for g in range(KV_HEADS):
        rs = slice(g * grp_rows, (g + 1) * grp_rows)
        acc = jnp.zeros((grp_rows, HEAD_DIM), F32)
        for p in range(np1):
            acc = acc + jnp.dot(pr[rs, p * PAGE_SIZE:(p + 1) * PAGE_SIZE].astype(BF16),
                                kv_tile(v_refs, p, g), preferred_element_type=F32)
        o_ref[rs, :] = acc / l[rs]


def _sample_bias(rel_bias, past, width):
    near = width - (past - PAGE_SIZE)
    b = _toeplitz_bias(rel_bias, SEQ_PAD, near, PAGE_SIZE).reshape(N_HEADS_A * SEQ_PAD, near)
    return jnp.pad(b, ((0, 0), (width - near, 0)))


def dsa_sample(qi8, w8, q8, kidx_new, k_new, v_new, cache_kidx, cache_k, cache_v, layer, page_table,
               rel_bias, t_valid):
    b, n_pages = page_table.shape
    np1 = n_pages + 1
    past = n_pages * PAGE_SIZE
    width = np1 * PAGE_SIZE
    topk = min(TOPK_MAX, (past + t_valid) // 4)
    bias = _sample_bias(rel_bias, past, width)
    rows = N_HEADS_A * SEQ_PAD
    kvw = KV_HEADS * HEAD_DIM

    def page_spec(tail, p):
        zeros = (0,) * len(tail)
        return pl.BlockSpec((None, None) + tail, lambda i, pt: (layer, pt[i, p]) + zeros)

    def new_spec(tail):
        return pl.BlockSpec((None,) + tail, lambda i, pt: (i, 0, 0))

    in_specs = [
        pl.BlockSpec((None, SEQ_PAD * IDX_HEADS, IDX_DIM), lambda i, pt: (i, 0, 0)),
        pl.BlockSpec((None, 1, SEQ_PAD * IDX_HEADS), lambda i, pt: (i, 0, 0)),
        pl.BlockSpec((None, rows, HEAD_DIM), lambda i, pt: (i, 0, 0)),
        pl.BlockSpec((rows, width), lambda i, pt: (0, 0)),
    ]
    args = [qi8, w8, q8, bias]
    kv_tail = (PAGE_SIZE * KV_HEADS, HEAD_DIM)
    cache_k = cache_k.reshape(cache_k.shape[:2] + kv_tail)
    cache_v = cache_v.reshape(cache_v.shape[:2] + kv_tail)
    kidx_tail = (IDX_DIM, PAGE_SIZE)
    cache_kidx = jnp.swapaxes(cache_kidx, 2, 3)
    kidx_new = jnp.swapaxes(kidx_new, 1, 2)
    new_tail = (PAGE_SIZE, kvw)
    for arr_cache, arr_new, tail, ntail in ((cache_kidx, kidx_new, kidx_tail, kidx_tail),
                                            (cache_k, k_new, kv_tail, new_tail), (cache_v, v_new, kv_tail, new_tail)):
        in_specs += [page_spec(tail, p) for p in range(n_pages)] + [new_spec(ntail)]
        args += [arr_cache] * n_pages + [arr_new]
    return pl.pallas_call(
        functools.partial(_dsa_sample_body, n_pages=n_pages, t_valid=t_valid, topk=topk),
        grid_spec=pltpu.PrefetchScalarGridSpec(
            num_scalar_prefetch=1,
            grid=(b,),
            in_specs=in_specs,
            out_specs=pl.BlockSpec((None, rows, HEAD_DIM), lambda i, pt: (i, 0, 0)),
            scratch_shapes=[pltpu.VMEM((SEQ_PAD, width), I32), pltpu.VMEM((rows, width), F32)],
        ),
        out_shape=jax.ShapeDtypeStruct((b, rows, HEAD_DIM), F32),
        compiler_params=_cparams(("arbitrary",), 48),
        name="dsa_sample",
    )(page_table, *args)


def _conv_body(x_ref, st_ref, w_ref, b_ref, o_ref, tail_ref, *, rows):
    @pl.when(pl.program_id(2) == 0)
    def _():
        tail_ref[...] = st_ref[...]

    x = x_ref[...]
    xc = jnp.concatenate([tail_ref[...], x], axis=0)
    out = b_ref[...] + x * w_ref[CONV_W - 1:CONV_W, :]
    for k in range(1, CONV_W):
        shifted = pltpu.roll(xc, k, 0)[SUBLANES:SUBLANES + rows]
        out = out + shifted * w_ref[CONV_W - 1 - k:CONV_W - k, :]
    o_ref[...] = out * _sigmoid(out)
    tail_ref[...] = x[rows - SUBLANES:rows]


def conv_silu(z3, col0, t, state8, conv_w, conv_b, rows, cdim, cb=512):
    b = z3.shape[0]
    assert col0 % cb == 0 and cdim % cb == 0 and t % rows == 0
    return pl.pallas_call(
        functools.partial(_conv_body, rows=rows),
        grid=(b, cdim // cb, t // rows),
        in_specs=[
            pl.BlockSpec((None, rows, cb), lambda i, j, c: (i, c, col0 // cb + j)),
            pl.BlockSpec((None, SUBLANES, cb), lambda i, j, c: (i, 0, j)),
            pl.BlockSpec((CONV_W, cb), lambda i, j, c: (0, j)),
            pl.BlockSpec((1, cb), lambda i, j, c: (0, j)),
        ],
        out_specs=pl.BlockSpec((None, rows, cb), lambda i, j, c: (i, c, j)),
        out_shape=jax.ShapeDtypeStruct((b, t, cdim), F32),
        scratch_shapes=[pltpu.VMEM((SUBLANES, cb), F32)],
        compiler_params=_cparams(("arbitrary", "arbitrary", "arbitrary"), 32),
        name="conv_silu",
    )(z3, state8, conv_w, conv_b.reshape(1, cdim))


def _ssd_pair(xs, dt_raw, dtb, a, dskip, cb, lmask, valid):
    dt = _softplus(dt_raw + dtb)
    if valid is not None:
        dt = jnp.where(valid, dt, 0.0)
    xdt = xs * dt
    acs = jnp.dot(jnp.where(lmask, 1.0, 0.0), dt * a, precision=HI, preferred_element_type=F32)
    acs_t = acs.T
    half = SSM_HEADDIM
    xdt_bf = xdt.astype(BF16)
    yd = []
    for lane0 in (0, half):
        seg = acs[:, lane0:lane0 + 1] - acs_t[lane0:lane0 + 1, :]
        lm = jnp.where(lmask, jnp.exp(jnp.where(lmask, seg, 0.0)), 0.0)
        yd.append(jnp.dot((cb * lm).astype(BF16), xdt_bf, preferred_element_type=F32))
    lane = lax.broadcasted_iota(I32, xs.shape, 1)
    y = jnp.where(lane < half, yd[0], yd[1]) + dskip * xs
    return xdt, acs, y


def _ssd_prompt_body(xs_ref, bm_ref, cm_ref, dt_ref, dtb_ref, a_ref, dsk_ref, y_ref, hf_ref, st_ref,
                     *, n_pairs, n_chunks):
    c = pl.program_id(0)

    @pl.when(c == 0)
    def _():
        st_ref[...] = jnp.zeros_like(st_ref)

    ll = SSD_L
    li = lax.broadcasted_iota(I32, (ll, ll), 0)
    si = lax.broadcasted_iota(I32, (ll, ll), 1)
    lmask = si <= li
    pairs_per_group = n_pairs // SSM_GROUPS
    for g in range(SSM_GROUPS):
        gs = slice(g * D_STATE, (g + 1) * D_STATE)
        bm = bm_ref[:, gs].astype(BF16)
        cm = cm_ref[:, gs].astype(BF16)
        cb = lax.dot_general(cm, bm, NT_DIMS, preferred_element_type=F32)
        for kk in range(pairs_per_group):
            k = g * pairs_per_group + kk
            ks = slice(k * LANES, (k + 1) * LANES)
            xdt, acs, y = _ssd_pair(xs_ref[:, ks], dt_ref[:, ks], dtb_ref[:, ks], a_ref[:, ks],
                                    dsk_ref[:, ks], cb, lmask, None)
            acs_last = acs[ll - 1:ll, :]
            state = st_ref[k]
            y_off = lax.dot_general(cm, state.astype(BF16), NT_DIMS, preferred_element_type=F32)
            y_ref[:, ks] = y + y_off * jnp.exp(acs)
            xd_t = (xdt * jnp.exp(acs_last - acs)).T
            upd = jnp.dot(xd_t.astype(BF16), bm, preferred_element_type=F32)
            cd = jnp.exp(jnp.broadcast_to(acs_last, (ll, LANES))).T[:, 0:1]
            st_ref[k] = state * cd + upd

    @pl.when(c == n_chunks - 1)
    def _():
        hf_ref[...] = st_ref[...]


def ssd_prompt(xc, dt_exp, dtb_exp, a_exp, dsk_exp, d_inner):
    t = xc.shape[0]
    n_pairs = d_inner // LANES
    gn = SSM_GROUPS * D_STATE
    n_chunks = t // SSD_L
    y, hf = pl.pallas_call(
        functools.partial(_ssd_prompt_body, n_pairs=n_pairs, n_chunks=n_chunks),
        grid=(n_chunks,),
        in_specs=[
            pl.BlockSpec((SSD_L, d_inner), lambda c: (c, 0)),
            pl.BlockSpec((SSD_L, gn), lambda c: (c, d_inner // gn)),
            pl.BlockSpec((SSD_L, gn), lambda c: (c, d_inner // gn + 1)),
            pl.BlockSpec((SSD_L, d_inner), lambda c: (c, 0)),
            _const_spec((1, d_inner)),
            _const_spec((1, d_inner)),
            _const_spec((1, d_inner)),
        ],
        out_specs=[
            pl.BlockSpec((SSD_L, d_inner), lambda c: (c, 0)),
            _const_spec((n_pairs, LANES, D_STATE)),
        ],
        out_shape=[
            jax.ShapeDtypeStruct((t, d_inner), F32),
            jax.ShapeDtypeStruct((n_pairs, LANES, D_STATE), F32),
        ],
        scratch_shapes=[pltpu.VMEM((n_pairs, LANES, D_STATE), F32)],
        compiler_params=_cparams(("arbitrary",), 32),
        name="ssd_prompt",
    )(xc, xc, xc, dt_exp, dtb_exp, a_exp, dsk_exp)
    return y, hf.reshape(2 * n_pairs, SSM_HEADDIM, D_STATE)


def _ssd_sample_body(xs_ref, bm_ref, cm_ref, dt_ref, dtb_ref, a_ref, dsk_ref, h0_ref, y_ref, h1_ref,
                     *, n_seq, t_valid):
    ll = n_seq * SEQ_PAD
    li = lax.broadcasted_iota(I32, (ll, ll), 0)
    si = lax.broadcasted_iota(I32, (ll, ll), 1)
    same = lax.shift_right_logical(li, 3) == lax.shift_right_logical(si, 3)
    lmask = same & (si <= li)
    last = same & ((si & (SEQ_PAD - 1)) == SEQ_PAD - 1)
    rowi = lax.broadcasted_iota(I32, (ll, LANES), 0)
    valid = (rowi & (SEQ_PAD - 1)) < t_valid
    bm = bm_ref[...].astype(BF16)
    cm = cm_ref[...].astype(BF16)
    cb = lax.dot_general(cm, bm, NT_DIMS, preferred_element_type=F32)
    xdt, acs, y = _ssd_pair(xs_ref[...], dt_ref[...], dtb_ref[...], a_ref[...], dsk_ref[...],
                            cb, lmask, valid)
    acs_last = jnp.dot(jnp.where(last, 1.0, 0.0), acs, precision=HI, preferred_element_type=F32)
    e_acs = jnp.exp(acs)
    xd_t = (xdt * jnp.exp(acs_last - acs)).T
    cd_t = jnp.exp(acs_last).T
    lane = lax.broadcasted_iota(I32, (LANES, ll), 1)
    cm32 = cm_ref[...]
    y_off = []
    for s in range(n_seq):
        rs = slice(s * SEQ_PAD, (s + 1) * SEQ_PAD)
        state = h0_ref[s].reshape(LANES, D_STATE)
        y_off.append(lax.dot_general(cm32[rs].astype(BF16), state.astype(BF16), NT_DIMS,
                                     preferred_element_type=F32) * e_acs[rs])
        in_seq = (lane >= s * SEQ_PAD) & (lane < (s + 1) * SEQ_PAD)
        upd = jnp.dot(jnp.where(in_seq, xd_t, 0.0).astype(BF16), bm, preferred_element_type=F32)
        new = state * cd_t[:, s * SEQ_PAD:s * SEQ_PAD + 1] + upd
        h1_ref[s] = new.reshape(2, SSM_HEADDIM, D_STATE)
    y_ref[...] = y + jnp.concatenate(y_off, axis=0)


def ssd_sample(xc, dt_exp, dtb_exp, a_exp, dsk_exp, h0, d_inner, t_valid, n_seq=16):
    rows = xc.shape[0]
    b = h0.shape[0]
    n_pairs = d_inner // LANES
    ppg = n_pairs // SSM_GROUPS
    ll = n_seq * SEQ_PAD
    first_b = d_inner // D_STATE
    return pl.pallas_call(
        functools.partial(_ssd_sample_body, n_seq=n_seq, t_valid=t_valid),
        grid=(b // n_seq, n_pairs),
        in_specs=[
            pl.BlockSpec((ll, LANES), lambda s, k: (s, k)),
            pl.BlockSpec((ll, D_STATE), lambda s, k: (s, first_b + k // ppg)),
            pl.BlockSpec((ll, D_STATE), lambda s, k: (s, first_b + SSM_GROUPS + k // ppg)),
            pl.BlockSpec((ll, LANES), lambda s, k: (s, k)),
            pl.BlockSpec((1, LANES), lambda s, k: (0, k)),
            pl.BlockSpec((1, LANES), lambda s, k: (0, k)),
            pl.BlockSpec((1, LANES), lambda s, k: (0, k)),
            pl.BlockSpec((n_seq, 2, SSM_HEADDIM, D_STATE), lambda s, k: (s, k, 0, 0)),
        ],
        out_specs=[
            pl.BlockSpec((ll, LANES), lambda s, k: (s, k)),
            pl.BlockSpec((n_seq, 2, SSM_HEADDIM, D_STATE), lambda s, k: (s, k, 0, 0)),
        ],
        out_shape=[
            jax.ShapeDtypeStruct((rows, d_inner), F32),
            jax.ShapeDtypeStruct(h0.shape, F32),
        ],
        compiler_params=_cparams(("arbitrary", "arbitrary"), 32),
        name="ssd_sample",
    )(xc, xc, xc, dt_exp, dtb_exp, a_exp, dsk_exp, h0)


def _gated_norm_body(y_ref, z_ref, g_ref, o_ref, *, d_inner):
    z = z_ref[...]
    yg = y_ref[...] * (z * _sigmoid(z))
    gw = d_inner // SSM_GROUPS
    for g in range(SSM_GROUPS):
        gs = slice(g * gw, (g + 1) * gw)
        v = yg[:, gs]
        r = lax.rsqrt(jnp.mean(v * v, axis=-1, keepdims=True) + EPS)
        o_ref[:, gs] = (v * r * g_ref[:, gs]).astype(BF16)


def gated_norm(y, z, norm_g, d_inner, tm=TM):
    t = y.shape[0]
    return pl.pallas_call(
        functools.partial(_gated_norm_body, d_inner=d_inner),
        grid=(t // tm,),
        in_specs=[
            pl.BlockSpec((tm, d_inner), lambda i: (i, 0)),
            pl.BlockSpec((tm, d_inner), lambda i: (i, 0)),
            _const_spec((1, d_inner)),
        ],
        out_specs=pl.BlockSpec((tm, d_inner), lambda i: (i, 0)),
        out_shape=jax.ShapeDtypeStruct((t, d_inner), BF16),
        compiler_params=_cparams(("arbitrary",), 32),
        name="gated_norm",
    )(y, z, norm_g.reshape(1, d_inner))


def _pack_cols(w, pieces, total):
    cols = [w[:, a:b] for a, b in pieces]
    used = sum(b - a for a, b in pieces)
    if total > used:
        cols.append(jnp.zeros((w.shape[0], total - used), w.dtype))
    return jnp.concatenate(cols, axis=1).astype(BF16)


def _pad_rows(a, rows, axis):
    pad = [(0, 0)] * a.ndim
    pad[axis] = (0, rows - a.shape[axis])
    return jnp.pad(a, pad)


def _dsa_layer(x_all, n_prompt, n_seq, t_s, mix_g, w_in, rel_bias, cache_k, cache_v, cache_kidx, layer,
               page_table):
    qw, kvw, qiw = N_HEADS_A * HEAD_DIM, KV_HEADS * HEAD_DIM, IDX_HEADS * IDX_DIM
    o_q, o_k, o_v, o_qi = 0, qw, qw + kvw, qw + 2 * kvw
    o_ki = o_qi + qiw
    o_wi = o_ki + IDX_DIM
    o_qm = o_wi + IDX_HEADS
    col = {"q": 0, "k": qw, "qi": qw + kvw, "v": qw + kvw + qiw, "qm": qw + 2 * kvw + qiw}
    col["kw"] = col["qm"] + MEM_WIDTH
    width = col["kw"] + LANES
    w = _pack_cols(w_in, [(o_q, o_q + qw), (o_k, o_k + kvw), (o_qi, o_qi + qiw), (o_v, o_v + kvw),
                          (o_qm, o_qm + MEM_WIDTH), (o_ki, o_ki + IDX_DIM), (o_wi, o_wi + IDX_HEADS)], width)
    z = norm_matmul(x_all, mix_g, w, TM, width // 3)

    k_all = z[:, col["k"]:col["k"] + kvw]
    v_all = z[:, col["v"]:col["v"] + kvw]
    ki_all = z[:, col["kw"]:col["kw"] + IDX_DIM]

    kiw_bf = z[:n_prompt, col["kw"]:col["kw"] + LANES].astype(BF16)
    mix_p = dsa_prompt(z, n_prompt, col, kiw_bf, k_all[:n_prompt].astype(BF16),
                       v_all[:n_prompt].astype(BF16), rel_bias)

    zs = z[n_prompt:].reshape(n_seq, t_s, width)
    zs8 = _pad_rows(zs, SEQ_PAD, 1)
    qi8 = zs8[:, :, col["qi"]:col["qi"] + qiw].reshape(n_seq, SEQ_PAD * IDX_HEADS, IDX_DIM)
    w8 = zs8[:, :, col["kw"] + IDX_DIM:col["kw"] + IDX_DIM + IDX_HEADS].reshape(n_seq, 1, SEQ_PAD * IDX_HEADS)
    q8 = zs8[:, :, :qw].reshape(n_seq, SEQ_PAD, N_HEADS_A, HEAD_DIM).transpose(0, 2, 1, 3)
    q8 = q8.reshape(n_seq, N_HEADS_A * SEQ_PAD, HEAD_DIM)
    kidx_new = _pad_rows(zs[:, :, col["kw"]:col["kw"] + IDX_DIM], PAGE_SIZE, 1)
    k_new = _pad_rows(zs[:, :, col["k"]:col["k"] + kvw], PAGE_SIZE, 1)
    v_new = _pad_rows(zs[:, :, col["v"]:col["v"] + kvw], PAGE_SIZE, 1)
    o_s = dsa_sample(qi8, w8, q8, kidx_new, k_new, v_new, cache_kidx, cache_k, cache_v, layer,
                     page_table, rel_bias, t_s)
    mix_s = o_s.reshape(n_seq, N_HEADS_A, SEQ_PAD, HEAD_DIM)[:, :, :t_s].transpose(0, 2, 1, 3)
    mix_s = mix_s.reshape(n_seq * t_s, qw).astype(BF16)
    return z, col["qm"] // MEM_WIDTH, jnp.concatenate([mix_p, mix_s], axis=0), k_all, v_all, ki_all


def _ssd_layer(x_all, n_prompt, n_seq, t_s, mix_g, w_in, conv_w, conv_b, dt_bias, a_log, d_skip, norm_g,
               state_conv, state_ssm):
    n_heads = dt_bias.shape[0]
    d_inner = n_heads * SSM_HEADDIM
    cdim = d_inner + 2 * SSM_GROUPS * D_STATE
    o_x, o_dt = d_inner, d_inner + cdim
    o_qm = o_dt + n_heads
    c_qm = d_inner + cdim
    c_dt = c_qm + MEM_WIDTH
    width = _round_up(c_dt + n_heads, 2 * LANES)
    w = _pack_cols(w_in, [(0, d_inner), (o_x, o_x + cdim), (o_qm, o_qm + MEM_WIDTH), (o_dt, o_dt + n_heads)], width)
    z = norm_matmul(x_all, mix_g, w, TM, width // 2)
    xbc = z[:, d_inner:d_inner + cdim]
    dt_exp = jnp.repeat(z[:, c_dt:c_dt + n_heads], SSM_HEADDIM, axis=1)
    dtb_exp = jnp.repeat(dt_bias.astype(F32), SSM_HEADDIM).reshape(1, d_inner)
    a_exp = -jnp.exp(jnp.repeat(a_log.astype(F32), SSM_HEADDIM)).reshape(1, d_inner)
    dsk_exp = jnp.repeat(d_skip.astype(F32), SSM_HEADDIM).reshape(1, d_inner)

    xc_p = conv_silu(z[None], d_inner, n_prompt, jnp.zeros((1, SUBLANES, cdim), F32), conv_w, conv_b,
                     8 * SSD_L, cdim)[0]
    y_p, hf_p = ssd_prompt(xc_p, dt_exp, dtb_exp, a_exp, dsk_exp, d_inner)
    new_conv_p = xbc[n_prompt - (CONV_W - 1):n_prompt]

    xbc_s = xbc[n_prompt:].reshape(n_seq, t_s, cdim)
    xbc_s8 = _pad_rows(xbc_s, SEQ_PAD, 1)
    st8 = jnp.concatenate([jnp.zeros((n_seq, SUBLANES - (CONV_W - 1), cdim), F32), state_conv.astype(F32)], axis=1)
    xc_s = conv_silu(xbc_s8, 0, SEQ_PAD, st8, conv_w, conv_b, SEQ_PAD, cdim, cb=cdim)
    xc_s = xc_s.reshape(n_seq * SEQ_PAD, cdim)
    dt_s8 = _pad_rows(dt_exp[n_prompt:].reshape(n_seq, t_s, d_inner), SEQ_PAD, 1).reshape(n_seq * SEQ_PAD, d_inner)
    y_s8, hf_s = ssd_sample(xc_s, dt_s8, dtb_exp, a_exp, dsk_exp, state_ssm.astype(F32), d_inner, t_s)
    y_s = y_s8.reshape(n_seq, SEQ_PAD, d_inner)[:, :t_s].reshape(n_seq * t_s, d_inner)
    new_conv_s = jnp.concatenate([state_conv.astype(F32), xbc_s], axis=1)[:, -(CONV_W - 1):]

    mix = gated_norm(jnp.concatenate([y_p, y_s], axis=0), z, norm_g, d_inner)
    return z, c_qm // MEM_WIDTH, mix, hf_p, new_conv_p, hf_s, new_conv_s


def kernel(x_prompt, x_sample, mem_prompt, cache_k, cache_v, cache_kidx, page_table, state_ssm, state_conv,
           cache_mem_k, cache_mem_v, rel_bias, ffn1_g, ffn1_w_gu, ffn1_w_down, mix_g, mem_g, w_mem_kv,
           w_in_attn, w_in_ssd, conv_w, conv_b, dt_bias, a_log, d_skip, ssd_norm_g, w_out,
           ffn2_g, ffn2_w_gu, ffn2_w_down, final_g):
    bp, n_prompt, d = x_prompt.shape
    n_seq, t_s, _ = x_sample.shape
    assert bp == 1
    depth = ffn1_g.shape[0]
    n_mem = mem_prompt.shape[1]
    x_all = jnp.concatenate([x_prompt[0], x_sample.reshape(n_seq * t_s, d)], axis=0)
    outs = {k: [] for k in ("pk", "pv", "pki", "pssm", "pconv", "pmk", "pmv", "sk", "sv", "ski", "sssm", "sconv")}
    y_all = None
    ffn1_w = prep_ffn_weights(ffn1_w_gu, ffn1_w_down, TF)
    ffn2_w = prep_ffn_weights(ffn2_w_gu, ffn2_w_down, TF)
    for i in range(depth):
        j = i // 2
        x_all = ffn(x_all, ffn1_g[i], ffn1_w, i)
        mkv = norm_matmul(mem_prompt[0], mem_g[i], w_mem_kv[i].astype(BF16), n_mem, 2 * MEM_WIDTH)
        outs["pmk"].append(mkv[:, :MEM_WIDTH].reshape(1, n_mem, MEM_HEADS, MEM_HEAD_DIM))
        outs["pmv"].append(mkv[:, MEM_WIDTH:].reshape(1, n_mem, MEM_HEADS, MEM_HEAD_DIM))
        if i % 2 == 0:
            z, qm_blk, mix, k_all, v_all, ki_all = _dsa_layer(
                x_all, n_prompt, n_seq, t_s, mix_g[i], w_in_attn[j], rel_bias,
                cache_k, cache_v, cache_kidx, j, page_table)
            outs["pk"].append(k_all[:n_prompt].reshape(1, n_prompt, KV_HEADS, HEAD_DIM))
            outs["pv"].append(v_all[:n_prompt].reshape(1, n_prompt, KV_HEADS, HEAD_DIM))
            outs["pki"].append(ki_all[:n_prompt].reshape(1, n_prompt, IDX_DIM))
            outs["sk"].append(k_all[n_prompt:].reshape(n_seq, t_s, KV_HEADS, HEAD_DIM))
            outs["sv"].append(v_all[n_prompt:].reshape(n_seq, t_s, KV_HEADS, HEAD_DIM))
            outs["ski"].append(ki_all[n_prompt:].reshape(n_seq, t_s, IDX_DIM))
        else:
            z, qm_blk, mix, hf_p, conv_p, hf_s, conv_s = _ssd_layer(
                x_all, n_prompt, n_seq, t_s, mix_g[i], w_in_ssd[j], conv_w[j], conv_b[j], dt_bias[j],
                a_log[j], d_skip[j], ssd_norm_g[j], state_conv[j], state_ssm[j])
            outs["pssm"].append(hf_p[None])
            outs["pconv"].append(conv_p[None])
            outs["sssm"].append(hf_s)
            outs["sconv"].append(conv_s)
        mem_p = mem_attn_prompt(z, qm_blk, n_prompt, mkv)
        qm_s = z[n_prompt:, qm_blk * MEM_WIDTH:(qm_blk + 1) * MEM_WIDTH].reshape(n_seq, t_s, MEM_WIDTH)
        mem_s = mem_attn_sample(_pad_rows(qm_s, SEQ_PAD, 1), cache_mem_k, cache_mem_v, i)
        mem = jnp.concatenate([mem_p, mem_s[:, :t_s].reshape(n_seq * t_s, MEM_WIDTH)], axis=0)
        x_all = out_proj(x_all, mix, mem, w_out[i])
        if i == depth - 1:
            x_all, y_all = ffn(x_all, ffn2_g[i], ffn2_w, i, final_g=final_g)
        else:
            x_all = ffn(x_all, ffn2_g[i], ffn2_w, i)
    y_prompt = y_all[:n_prompt][None]
    y_sample = y_all[n_prompt:].reshape(n_seq, t_s, d)
    st = lambda k: jnp.stack(outs[k])
    return (y_prompt, y_sample, st("pk"), st("pv"), st("pki"), st("pssm"), st("pconv"), st("pmk"), st("pmv"),
            st("sk"), st("sv"), st("ski"), st("sssm"), st("sconv"))
```

```python
import functools
import math

import numpy as np
import jax
import jax.numpy as jnp
from jax import lax
from jax.experimental import pallas as pl
from jax.experimental.pallas import tpu as pltpu

F32, BF16, I32 = jnp.float32, jnp.bfloat16, jnp.int32
HI = lax.Precision.HIGHEST
NT_DIMS = (((1,), (1,)), ((), ()))

HEAD_DIM = 128
KV_HEADS = 4
Q_PER_KV = 3
N_HEADS_A = KV_HEADS * Q_PER_KV
IDX_HEADS = 16
IDX_DIM = 64
TOPK_MAX = 256
REL_BUCKETS = 32
REL_MAX_EXACT = 16
REL_MAX_DIST = 128
MEM_HEADS = 4
MEM_HEAD_DIM = 128
MEM_WIDTH = MEM_HEADS * MEM_HEAD_DIM
SSM_HEADDIM = 64
SSM_GROUPS = 4
D_STATE = 128
CONV_W = 4
PAGE_SIZE = 128
EPS = 1e-6

LANES = 128
SUBLANES = 8
NEG = -1e30
INT_MIN = -2 ** 31
IDX_SCALE = IDX_DIM ** -0.5 * IDX_HEADS ** -0.5
ATT_SCALE = HEAD_DIM ** -0.5
MEM_SCALE = MEM_HEAD_DIM ** -0.5

QB = 128
KC = 512
TM = 512
TF = 512
SSD_L = 128
SEQ_PAD = 8


def _cparams(sem, vmem_mb):
    return pltpu.CompilerParams(dimension_semantics=sem, vmem_limit_bytes=vmem_mb * 2 ** 20)


def _round_up(n, m):
    return (n + m - 1) // m * m


def _const_spec(shape):
    nd = len(shape)
    return pl.BlockSpec(shape, lambda *_: (0,) * nd)


def _resident_spec(shape):
    nd = len(shape)
    return pl.BlockSpec(shape, lambda *_: (0,) * nd, pipeline_mode=pl.Buffered(1))


def _rms(x, g):
    return x * lax.rsqrt(jnp.mean(x * x, axis=-1, keepdims=True) + EPS) * g


def _sigmoid(x):
    return 1.0 / (1.0 + jnp.exp(-x))


def _softplus(x):
    return jnp.maximum(x, 0.0) + jnp.log1p(jnp.exp(-jnp.abs(x)))


def _ffn_body(x_ref, g_ref, wg_ref, wu_ref, wd_ref, *rest, nj, with_final):
    if with_final:
        fg_ref, o_ref, y_ref, xn_ref, acc_ref = rest
    else:
        o_ref, xn_ref, acc_ref = rest
    j = pl.program_id(1)

    @pl.when(j == 0)
    def _():
        xn_ref[...] = _rms(x_ref[...], g_ref[...]).astype(BF16)
        acc_ref[...] = jnp.zeros_like(acc_ref)

    xn = xn_ref[...]
    gate = jnp.dot(xn, wg_ref[...], preferred_element_type=F32)
    up = jnp.dot(xn, wu_ref[...], preferred_element_type=F32)
    a = gate * _sigmoid(gate) * up
    acc_ref[...] += jnp.dot(a.astype(BF16), wd_ref[...], preferred_element_type=F32)

    @pl.when(j == nj - 1)
    def _():
        o = x_ref[...] + 0.5 * acc_ref[...]
        o_ref[...] = o
        if with_final:
            y_ref[...] = _rms(o, fg_ref[...])


def _cast_gate_up_body(g_ref, u_ref, og_ref, ou_ref, *, n_valid):
    keep = pl.program_id(1) < n_valid
    og_ref[...] = jnp.where(keep, g_ref[...], 0.0).astype(BF16)
    ou_ref[...] = jnp.where(keep, u_ref[...], 0.0).astype(BF16)


def _cast_down_body(w_ref, o_ref, *, n_valid):
    o_ref[...] = jnp.where(pl.program_id(1) < n_valid, w_ref[...], 0.0).astype(BF16)


def prep_ffn_weights(w_gu, w_d, tf):
    nl, d, two_ff = w_gu.shape
    ff = two_ff // 2
    ffp = _round_up(ff, tf)
    assert ff % LANES == 0
    nb, nbp = ff // LANES, ffp // LANES
    wg, wu = pl.pallas_call(
        functools.partial(_cast_gate_up_body, n_valid=nb),
        grid=(nl, nbp),
        in_specs=[pl.BlockSpec((None, d, LANES), lambda l, j: (l, 0, jnp.minimum(j, nb - 1))),
                  pl.BlockSpec((None, d, LANES), lambda l, j: (l, 0, nb + jnp.minimum(j, nb - 1)))],
        out_specs=[pl.BlockSpec((None, d, LANES), lambda l, j: (l, 0, j))] * 2,
        out_shape=[jax.ShapeDtypeStruct((nl, d, ffp), BF16)] * 2,
        compiler_params=_cparams(("arbitrary", "arbitrary"), 32),
        name="cast_gate_up",
    )(w_gu, w_gu)
    wd = pl.pallas_call(
        functools.partial(_cast_down_body, n_valid=nb),
        grid=(nl, nbp),
        in_specs=[pl.BlockSpec((None, LANES, d), lambda l, j: (l, jnp.minimum(j, nb - 1), 0))],
        out_specs=pl.BlockSpec((None, LANES, d), lambda l, j: (l, j, 0)),
        out_shape=jax.ShapeDtypeStruct((nl, ffp, d), BF16),
        compiler_params=_cparams(("arbitrary", "arbitrary"), 32),
        name="cast_down",
    )(w_d)
    return wg, wu, wd


def ffn(x, g, weights, layer, final_g=None, tm=TM, tf=TF):
    t, d = x.shape
    wg, wu, wd = weights
    nj = wg.shape[2] // tf
    with_final = final_g is not None
    in_specs = [
        pl.BlockSpec((tm, d), lambda i, j: (i, 0)),
        _const_spec((1, d)),
        pl.BlockSpec((None, d, tf), lambda i, j: (layer, 0, j)),
        pl.BlockSpec((None, d, tf), lambda i, j: (layer, 0, j)),
        pl.BlockSpec((None, tf, d), lambda i, j: (layer, j, 0)),
    ]
    args = [x, g.reshape(1, d), wg, wu, wd]
    out_shape = [jax.ShapeDtypeStruct((t, d), F32)]
    out_specs = [pl.BlockSpec((tm, d), lambda i, j: (i, 0))]
    if with_final:
        in_specs.append(_const_spec((1, d)))
        args.append(final_g.reshape(1, d))
        out_shape.append(jax.ShapeDtypeStruct((t, d), F32))
        out_specs.append(pl.BlockSpec((tm, d), lambda i, j: (i, 0)))
    res = pl.pallas_call(
        functools.partial(_ffn_body, nj=nj, with_final=with_final),
        grid=(t // tm, nj),
        in_specs=in_specs,
        out_specs=out_specs,
        out_shape=out_shape,
        scratch_shapes=[pltpu.VMEM((tm, d), BF16), pltpu.VMEM((tm, d), F32)],
        compiler_params=_cparams(("arbitrary", "arbitrary"), 56),
        name="ffn",
    )(*args)
    return res if with_final else res[0]


def _norm_matmul_body(x_ref, g_ref, w_ref, o_ref, xn_ref):
    @pl.when(pl.program_id(1) == 0)
    def _():
        xn_ref[...] = _rms(x_ref[...], g_ref[...]).astype(BF16)

    o_ref[...] = jnp.dot(xn_ref[...], w_ref[...], preferred_element_type=F32)


def norm_matmul(x, g, w_bf16, tm, tn):
    t, d = x.shape
    n = w_bf16.shape[1]
    return pl.pallas_call(
        _norm_matmul_body,
        grid=(t // tm, n // tn),
        in_specs=[
            pl.BlockSpec((tm, d), lambda i, j: (i, 0)),
            _const_spec((1, d)),
            pl.BlockSpec((d, tn), lambda i, j: (0, j)),
        ],
        out_specs=pl.BlockSpec((tm, tn), lambda i, j: (i, j)),
        out_shape=jax.ShapeDtypeStruct((t, n), F32),
        scratch_shapes=[pltpu.VMEM((tm, d), BF16)],
        compiler_params=_cparams(("arbitrary", "arbitrary"), 48),
        name="norm_matmul",
    )(x, g.reshape(1, d), w_bf16)


def _out_proj_body(x_ref, mix_ref, mem_ref, w1_ref, w2_ref, o_ref):
    o_ref[...] = (x_ref[...]
                  + jnp.dot(mix_ref[...], w1_ref[...], preferred_element_type=F32)
                  + jnp.dot(mem_ref[...], w2_ref[...], preferred_element_type=F32))


def out_proj(x, mix, mem, w_out, tm=TM):
    t, d = x.shape
    dm, dw = mix.shape[1], mem.shape[1]
    w1 = w_out[:dm].astype(BF16)
    w2 = w_out[dm:].astype(BF16)
    return pl.pallas_call(
        _out_proj_body,
        grid=(t // tm,),
        in_specs=[
            pl.BlockSpec((tm, d), lambda i: (i, 0)),
            pl.BlockSpec((tm, dm), lambda i: (i, 0)),
            pl.BlockSpec((tm, dw), lambda i: (i, 0)),
            _const_spec((dm, d)),
            _const_spec((dw, d)),
        ],
        out_specs=pl.BlockSpec((tm, d), lambda i: (i, 0)),
        out_shape=jax.ShapeDtypeStruct((t, d), F32),
        compiler_params=_cparams(("arbitrary",), 48),
        name="out_proj",
    )(x, mix, mem, w1, w2)


def _softmax_rows(s):
    m = jnp.max(s, axis=-1, keepdims=True)
    p = jnp.exp(s - m)
    return p, jnp.sum(p, axis=-1, keepdims=True)


def _mem_attn_prompt_body(q_ref, mk_ref, mv_ref, o_ref):
    for h in range(MEM_HEADS):
        sl = slice(h * MEM_HEAD_DIM, (h + 1) * MEM_HEAD_DIM)
        q = (q_ref[:, sl] * MEM_SCALE).astype(BF16)
        s = lax.dot_general(q, mk_ref[:, sl].astype(BF16), NT_DIMS, preferred_element_type=F32)
        p, l = _softmax_rows(s)
        o = jnp.dot(p.astype(BF16), mv_ref[:, sl].astype(BF16), preferred_element_type=F32)
        o_ref[:, sl] = (o / l).astype(BF16)


def mem_attn_prompt(z, qm_col_block, n_rows, mkv, tm=TM):
    m = mkv.shape[0]
    return pl.pallas_call(
        _mem_attn_prompt_body,
        grid=(n_rows // tm,),
        in_specs=[
            pl.BlockSpec((tm, MEM_WIDTH), lambda i: (i, qm_col_block)),
            pl.BlockSpec((m, MEM_WIDTH), lambda i: (0, 0)),
            pl.BlockSpec((m, MEM_WIDTH), lambda i: (0, 1)),
        ],
        out_specs=pl.BlockSpec((tm, MEM_WIDTH), lambda i: (i, 0)),
        out_shape=jax.ShapeDtypeStruct((n_rows, MEM_WIDTH), BF16),
        compiler_params=_cparams(("arbitrary",), 32),
        name="mem_attn_prompt",
    )(z, mkv, mkv)


def _mem_attn_sample_body(q_ref, k_ref, v_ref, o_ref, *, bb, m):
    rows, cols = MEM_HEADS * SEQ_PAD, m * MEM_HEADS
    row_head = lax.shift_right_logical(lax.broadcasted_iota(I32, (rows, cols), 0), SEQ_PAD.bit_length() - 1)
    col_head = lax.broadcasted_iota(I32, (rows, cols), 1) & (MEM_HEADS - 1)
    madd = jnp.where(row_head == col_head, 0.0, NEG)
    for b in range(bb):
        q = jnp.concatenate([q_ref[b, :, h * MEM_HEAD_DIM:(h + 1) * MEM_HEAD_DIM] for h in range(MEM_HEADS)], axis=0)
        s = lax.dot_general((q * MEM_SCALE).astype(BF16), k_ref[b].astype(BF16), NT_DIMS,
                            preferred_element_type=F32)
        p, l = _softmax_rows(s + madd)
        o = jnp.dot(p.astype(BF16), v_ref[b].astype(BF16), preferred_element_type=F32) / l
        for h in range(MEM_HEADS):
            o_ref[b, :, h * MEM_HEAD_DIM:(h + 1) * MEM_HEAD_DIM] = o[h * SEQ_PAD:(h + 1) * SEQ_PAD].astype(BF16)


def mem_attn_sample(qm8, mem_k, mem_v, layer, bb=8):
    depth, b, m, nh, hd = mem_k.shape
    mem_k = mem_k.reshape(depth, b, m * nh, hd)
    mem_v = mem_v.reshape(depth, b, m * nh, hd)
    cache_spec = pl.BlockSpec((None, bb, m * nh, hd), lambda i: (layer, i, 0, 0))
    return pl.pallas_call(
        functools.partial(_mem_attn_sample_body, bb=bb, m=m),
        grid=(b // bb,),
        in_specs=[pl.BlockSpec((bb, SEQ_PAD, MEM_WIDTH), lambda i: (i, 0, 0)), cache_spec, cache_spec],
        out_specs=pl.BlockSpec((bb, SEQ_PAD, MEM_WIDTH), lambda i: (i, 0, 0)),
        out_shape=jax.ShapeDtypeStruct((b, SEQ_PAD, MEM_WIDTH), BF16),
        compiler_params=_cparams(("arbitrary",), 40),
        name="mem_attn_sample",
    )(qm8, mem_k, mem_v)


def _sortable_key(x):
    b = pltpu.bitcast(x, I32)
    return jnp.where(b < 0, (b ^ 0x7FFFFFFF) + 1, b)


def _t5_bucket_np(dist):
    n = np.maximum(dist, 0)
    nf = np.maximum(n, 1).astype(np.float64)
    large = REL_MAX_EXACT + (np.log(nf / REL_MAX_EXACT) / math.log(REL_MAX_DIST / REL_MAX_EXACT)
                             * (REL_BUCKETS - REL_MAX_EXACT)).astype(np.int32)
    large = np.minimum(large, REL_BUCKETS - 1)
    return np.where(n < REL_MAX_EXACT, n, large)


FAR_DIST = int(np.min(np.nonzero(_t5_bucket_np(np.arange(4 * REL_MAX_DIST)) == REL_BUCKETS - 1)[0]))
assert np.all(_t5_bucket_np(np.arange(FAR_DIST, 1 << 16)) == REL_BUCKETS - 1) and FAR_DIST <= QB


def _toeplitz_bias(rel_bias, n, m, k0):
    p = n + m
    d = k0 + (n - 1) - np.arange(p)
    b = _t5_bucket_np(d)
    keep = (d >= 0) & (b != REL_BUCKETS - 1)
    u = jnp.where(keep[:, None], rel_bias[b] - rel_bias[REL_BUCKETS - 1], 0.0).T
    skew = jnp.tile(u, (1, n))[:, :n * (p - 1)].reshape(-1, n, p - 1)
    return skew[:, :, n - 1:n - 1 + m].astype(F32)


def _threshold_search(count_ge, count_tie_lt, shape, topk, n_idx_bits, all_idx):
    zero = jnp.zeros(shape, I32)
    c0 = count_ge(zero)
    t = jnp.where(c0 >= topk, zero, jnp.full(shape, INT_MIN, I32))
    n_ge = jnp.where(c0 >= topk, c0, -NEG)

    def bit_body(b, carry):
        t, n_ge = carry
        cand = t | lax.shift_left(jnp.int32(1), 30 - b)
        c = count_ge(cand)
        ok = c >= topk
        return jnp.where(ok, cand, t), jnp.where(ok, c, n_ge)

    t, n_ge = lax.fori_loop(0, 31, bit_body, (t, n_ge))
    has_k = t > INT_MIN
    excess = jnp.max(jnp.where(has_k, n_ge, 0.0)) > topk

    def tie_search(_):
        need = topk - count_ge(t + 1)

        def jbit(b, j):
            cand = j | lax.shift_left(jnp.int32(1), n_idx_bits - 1 - b)
            return jnp.where(count_tie_lt(t, cand) < need, cand, j)
        return lax.fori_loop(0, n_idx_bits, jbit, jnp.zeros(shape, I32))

    j = lax.cond(excess, tie_search, lambda _: jnp.full(shape, all_idx, I32), 0)
    j = jnp.where(has_k, j, -1)
    return t, j


def _dsa_prompt_body(q_ref, qi_ref, kw_ref, kiw_ref, k_ref, v_ref, bias_ref, o_ref,
                     rhs_ref, wt_ref, key_ref, qe_ref, m_ref, acc_ref, p_ref, *, seq, topk):
    i = pl.program_id(0)
    q0 = i * QB

    qi_t = qi_ref[...].T
    pad = jnp.zeros((LANES - IDX_DIM, QB), F32)
    for h in range(IDX_HEADS):
        blk = jnp.concatenate([qi_t[h * IDX_DIM:(h + 1) * IDX_DIM], pad], axis=0)
        rhs_ref[:, h * QB:(h + 1) * QB] = blk.astype(BF16)
    wt_ref[...] = kw_ref[...].T * IDX_SCALE

    n_chunks = (jnp.maximum(i + 1, 2) * QB + KC - 1) // KC

    def score_chunk(c, carry):
        c0 = pl.multiple_of(c * KC, KC)
        x = jnp.dot(kiw_ref[pl.ds(c0, KC), :], rhs_ref[...], preferred_element_type=F32)
        acc = jnp.zeros((KC, QB), F32)
        for h in range(IDX_HEADS):
            acc = acc + jnp.maximum(x[:, h * QB:(h + 1) * QB], 0.0) * wt_ref[IDX_DIM + h:IDX_DIM + h + 1, :]
        kpos = c0 + lax.broadcasted_iota(I32, (KC, QB), 0)
        qpos = q0 + lax.broadcasted_iota(I32, (KC, QB), 1)
        key_ref[pl.ds(c0, KC), :] = jnp.where(kpos <= qpos, _sortable_key(acc), INT_MIN)
        return carry

    lax.fori_loop(0, n_chunks, score_chunk, 0)

    def column_count(hit_of_chunk):
        acc_rows = 8 * SUBLANES

        def body(c, a):
            c0 = pl.multiple_of(c * KC, KC)
            hit = jnp.where(hit_of_chunk(key_ref[pl.ds(c0, KC), :], c0), 1.0, 0.0)
            return a + hit.reshape(KC // acc_rows, acc_rows, QB).sum(axis=0)

        a = lax.fori_loop(0, n_chunks, body, jnp.zeros((acc_rows, QB), F32))
        return jnp.sum(a, axis=0, keepdims=True)

    def count_ge(cand):
        return column_count(lambda kk, c0: kk >= cand)

    def count_tie_lt(t, jc):
        row = lax.broadcasted_iota(I32, (KC, QB), 0)
        return column_count(lambda kk, c0: (kk == t) & (c0 + row < jc))

    t, j = _threshold_search(count_ge, count_tie_lt, (1, QB), topk, (seq - 1).bit_length(), seq)

    eye = jnp.where(lax.broadcasted_iota(I32, (QB, QB), 0) == lax.broadcasted_iota(I32, (QB, QB), 1),
                    1.0, 0.0).astype(BF16)
    for g in range(KV_HEADS):
        for r in range(Q_PER_KV):
            hq = g * Q_PER_KV + r
            rs = slice(r * QB, (r + 1) * QB)
            qe_ref[g, rs, 0:HEAD_DIM] = (q_ref[:, hq * HEAD_DIM:(hq + 1) * HEAD_DIM] * ATT_SCALE).astype(BF16)
            qe_ref[g, rs, HEAD_DIM:2 * HEAD_DIM] = eye
    m_ref[...] = jnp.full(m_ref.shape, 3 * NEG, F32)
    acc_ref[...] = jnp.zeros(acc_ref.shape, F32)

    def mask_t(c0, width, far_end):
        kk = key_ref[pl.ds(c0, width), :]
        kpos = c0 + lax.broadcasted_iota(I32, (width, QB), 0)
        sel = (kk > t) | ((kk == t) & (kpos <= j))
        if far_end is not None:
            sel = sel & (kpos < far_end)
        return jnp.where(sel, 0.0, NEG).astype(BF16)

    def add_pv(g, p, c0, width):
        gs = slice(g * HEAD_DIM, (g + 1) * HEAD_DIM)
        v_ext = jnp.concatenate([v_ref[pl.ds(c0, width), gs], jnp.ones((width, HEAD_DIM), BF16)], axis=1)
        acc_ref[g] = acc_ref[g] + jnp.dot(p, v_ext, preferred_element_type=F32)

    def logits(g, c0, width, madd_t):
        gs = slice(g * HEAD_DIM, (g + 1) * HEAD_DIM)
        k_ext = jnp.concatenate([k_ref[pl.ds(c0, width), gs], madd_t], axis=1)
        return lax.dot_general(qe_ref[g], k_ext, NT_DIMS, preferred_element_type=F32)

    def probs(g, s):
        m_old = m_ref[g]
        m_new = jnp.maximum(m_old, jnp.max(s, axis=1, keepdims=True))
        alpha = jnp.exp(m_old - m_new)
        acc_ref[g] = jnp.concatenate([alpha, alpha], axis=1) * acc_ref[g]
        m_ref[g] = m_new
        return jnp.exp((s - m_new[:, 0:1]).astype(BF16))

    far_end = jnp.maximum(i - 1, 0) * QB
    n_far = (far_end + KC - 1) // KC
    last0 = pl.multiple_of(jnp.maximum(n_far - 1, 0) * KC, KC)
    p_ref[...] = jnp.zeros(p_ref.shape, BF16)

    def far_body(c, carry):
        c0 = pl.multiple_of(c * KC, KC)
        prev0 = pl.multiple_of(jnp.maximum(c - 1, 0) * KC, KC)
        madd_t = mask_t(c0, KC, far_end)
        for g in range(KV_HEADS):
            add_pv(g, p_ref[g], prev0, KC)
            p_ref[g] = probs(g, logits(g, c0, KC, madd_t))
        return carry

    lax.fori_loop(0, n_far, far_body, 0)
    near0 = pl.multiple_of(far_end, QB)
    variant = jnp.minimum(i, 1)
    madd_near = mask_t(near0, 2 * QB, None)
    for g in range(KV_HEADS):
        add_pv(g, p_ref[g], last0, KC)
        s = logits(g, near0, 2 * QB, madd_near) + bias_ref[variant, g]
        add_pv(g, probs(g, s), near0, 2 * QB)

    for g in range(KV_HEADS):
        o = acc_ref[g, :, 0:HEAD_DIM] / acc_ref[g, :, HEAD_DIM:2 * HEAD_DIM]
        for r in range(Q_PER_KV):
            hq = g * Q_PER_KV + r
            o_ref[:, hq * HEAD_DIM:(hq + 1) * HEAD_DIM] = o[r * QB:(r + 1) * QB].astype(BF16)


def _prompt_bias(rel_bias):
    out = [_toeplitz_bias(rel_bias, QB, 2 * QB, k0).reshape(KV_HEADS, Q_PER_KV * QB, 2 * QB) for k0 in (0, QB)]
    return jnp.stack(out)


def dsa_prompt(z, seq, col, kiw_bf, k_bf, v_bf, rel_bias):
    topk = min(TOPK_MAX, seq // 4)
    assert seq % KC == 0 and seq >= 2 * QB
    qw = N_HEADS_A * HEAD_DIM
    qiw = IDX_HEADS * IDX_DIM
    bias = _prompt_bias(rel_bias)
    rows = Q_PER_KV * QB
    return pl.pallas_call(
        functools.partial(_dsa_prompt_body, seq=seq, topk=topk),
        grid=(seq // QB,),
        in_specs=[
            pl.BlockSpec((QB, qw), lambda i: (i, col["q"] // qw)),
            pl.BlockSpec((QB, qiw), lambda i: (i, col["qi"] // qiw)),
            pl.BlockSpec((QB, LANES), lambda i: (i, col["kw"] // LANES)),
            _resident_spec((seq, LANES)),
            _resident_spec((seq, KV_HEADS * HEAD_DIM)),
            _resident_spec((seq, KV_HEADS * HEAD_DIM)),
            _resident_spec(bias.shape),
        ],
        out_specs=pl.BlockSpec((QB, qw), lambda i: (i, 0)),
        out_shape=jax.ShapeDtypeStruct((seq, qw), BF16),
        scratch_shapes=[
            pltpu.VMEM((LANES, IDX_HEADS * QB), BF16),
            pltpu.VMEM((LANES, QB), F32),
            pltpu.VMEM((seq, QB), I32),
            pltpu.VMEM((KV_HEADS, rows, 2 * HEAD_DIM), BF16),
            pltpu.VMEM((KV_HEADS, rows, LANES), F32),
            pltpu.VMEM((KV_HEADS, rows, 2 * HEAD_DIM), F32),
            pltpu.VMEM((KV_HEADS, rows, KC), BF16),
        ],
        compiler_params=_cparams(("arbitrary",), 56),
        name="dsa_prompt",
    )(z, z, z, kiw_bf, k_bf, v_bf, bias)


def _dsa_sample_body(pt_ref, qi_ref, w_ref, q_ref, bias_ref, *refs, n_pages, ns, t_valid, topk):
    del pt_ref
    np1 = n_pages + 1
    n_in = ns * n_pages + 1
    kidx_refs, k_refs, v_refs = refs[0:n_in], refs[n_in:2 * n_in], refs[2 * n_in:3 * n_in]
    o_ref, key_ref, s_ref = refs[3 * n_in:]
    past = n_pages * PAGE_SIZE
    width = np1 * PAGE_SIZE
    rows = KV_HEADS * Q_PER_KV * SEQ_PAD
    n_th = SEQ_PAD * IDX_HEADS

    for s in range(ns):
        qi = qi_ref[s].astype(BF16)
        w = jnp.broadcast_to(w_ref[s] * IDX_SCALE, (n_th, n_th)).T[:, 0:1]
        for p in range(np1):
            page = kidx_refs[s * n_pages + p][...] if p < n_pages else kidx_refs[n_in - 1][s]
            x = jnp.dot(qi, page.astype(BF16), preferred_element_type=F32)
            sc = (jnp.maximum(x, 0.0) * w).reshape(SEQ_PAD, IDX_HEADS, PAGE_SIZE).sum(axis=1)
            kpos = p * PAGE_SIZE + lax.broadcasted_iota(I32, (SEQ_PAD, PAGE_SIZE), 1)
            qpos = past + lax.broadcasted_iota(I32, (SEQ_PAD, PAGE_SIZE), 0)
            ok = (kpos <= qpos) & (kpos < past + t_valid)
            key_ref[s * SEQ_PAD:(s + 1) * SEQ_PAD, p * PAGE_SIZE:(p + 1) * PAGE_SIZE] = jnp.where(
                ok, _sortable_key(sc), INT_MIN)

    keys = key_ref[...]
    kpos = lax.broadcasted_iota(I32, (ns * SEQ_PAD, width), 1)

    def count_ge(cand):
        return jnp.sum(jnp.where(keys >= cand, 1.0, 0.0), axis=1, keepdims=True)

    def count_tie_lt(t, jc):
        return jnp.sum(jnp.where((keys == t) & (kpos < jc), 1.0, 0.0), axis=1, keepdims=True)

    t, j = _threshold_search(count_ge, count_tie_lt, (ns * SEQ_PAD, 1), topk, (width - 1).bit_length(), width)
    sel = (keys > t) | ((keys == t) & (kpos <= j))
    madd_all = jnp.where(sel, 0.0, NEG)

    def kv_tile(page_refs, s, p, g):
        if p < n_pages:
            return page_refs[s * n_pages + p][pl.ds(g, PAGE_SIZE, stride=KV_HEADS), :].astype(BF16)
        return page_refs[n_in - 1][s, :, g * HEAD_DIM:(g + 1) * HEAD_DIM].astype(BF16)

    grp_rows = Q_PER_KV * SEQ_PAD
    for s in range(ns):
        madd = jnp.concatenate([madd_all[s * SEQ_PAD:(s + 1) * SEQ_PAD]] * (KV_HEADS * Q_PER_KV), axis=0)
        for g in range(KV_HEADS):
            rs = slice(g * grp_rows, (g + 1) * grp_rows)
            qg = (q_ref[s, rs, :] * ATT_SCALE).astype(BF16)
            for p in range(np1):
                s_ref[s, rs, p * PAGE_SIZE:(p + 1) * PAGE_SIZE] = lax.dot_general(
                    qg, kv_tile(k_refs, s, p, g), NT_DIMS, preferred_element_type=F32)
        pr, l = _softmax_rows(s_ref[s] + bias_ref[...] + madd)
        for g in range(KV_HEADS):
            rs = slice(g * grp_rows, (g + 1) * grp_rows)
            acc = jnp.zeros((grp_rows, HEAD_DIM), F32)
            for p in range(np1):
                acc = acc + jnp.dot(pr[rs, p * PAGE_SIZE:(p + 1) * PAGE_SIZE].astype(BF16),
                                    kv_tile(v_refs, s, p, g), preferred_element_type=F32)
            o_ref[s, rs, :] = acc / l[rs]


def _sample_bias(rel_bias, past, width):
    near = width - (past - PAGE_SIZE)
    b = _toeplitz_bias(rel_bias, SEQ_PAD, near, PAGE_SIZE).reshape(N_HEADS_A * SEQ_PAD, near)
    return jnp.pad(b, ((0, 0), (width - near, 0)))


def dsa_sample(qi8, w8, q8, kidx_new, k_new, v_new, cache_kidx, cache_k, cache_v, layer, page_table,
               rel_bias, t_valid, ns=2):
    b, n_pages = page_table.shape
    np1 = n_pages + 1
    past = n_pages * PAGE_SIZE
    width = np1 * PAGE_SIZE
    topk = min(TOPK_MAX, (past + t_valid) // 4)
    bias = _sample_bias(rel_bias, past, width)
    rows = N_HEADS_A * SEQ_PAD
    kvw = KV_HEADS * HEAD_DIM

    assert b % ns == 0

    def page_spec(tail, s, p):
        zeros = (0,) * len(tail)
        return pl.BlockSpec((None, None) + tail, lambda i, pt: (layer, pt[i * ns + s, p]) + zeros)

    def new_spec(tail):
        return pl.BlockSpec((ns,) + tail, lambda i, pt: (i, 0, 0))

    in_specs = [
        pl.BlockSpec((ns, SEQ_PAD * IDX_HEADS, IDX_DIM), lambda i, pt: (i, 0, 0)),
        pl.BlockSpec((ns, 1, SEQ_PAD * IDX_HEADS), lambda i, pt: (i, 0, 0)),
        pl.BlockSpec((ns, rows, HEAD_DIM), lambda i, pt: (i, 0, 0)),
        pl.BlockSpec((rows, width), lambda i, pt: (0, 0)),
    ]
    args = [qi8, w8, q8, bias]
    kv_tail = (PAGE_SIZE * KV_HEADS, HEAD_DIM)
    cache_k = cache_k.reshape(cache_k.shape[:2] + kv_tail)
    cache_v = cache_v.reshape(cache_v.shape[:2] + kv_tail)
    kidx_tail = (IDX_DIM, PAGE_SIZE)
    cache_kidx = jnp.swapaxes(cache_kidx, 2, 3)
    kidx_new = jnp.swapaxes(kidx_new, 1, 2)
    new_tail = (PAGE_SIZE, kvw)
    for arr_cache, arr_new, tail, ntail in ((cache_kidx, kidx_new, kidx_tail, kidx_tail),
                                            (cache_k, k_new, kv_tail, new_tail), (cache_v, v_new, kv_tail, new_tail)):
        in_specs += [page_spec(tail, s, p) for s in range(ns) for p in range(n_pages)] + [new_spec(ntail)]
        args += [arr_cache] * (ns * n_pages) + [arr_new]
    return pl.pallas_call(
        functools.partial(_dsa_sample_body, n_pages=n_pages, ns=ns, t_valid=t_valid, topk=topk),
        grid_spec=pltpu.PrefetchScalarGridSpec(
            num_scalar_prefetch=1,
            grid=(b // ns,),
            in_specs=in_specs,
            out_specs=pl.BlockSpec((ns, rows, HEAD_DIM), lambda i, pt: (i, 0, 0)),
            scratch_shapes=[pltpu.VMEM((ns * SEQ_PAD, width), I32), pltpu.VMEM((ns, rows, width), F32)],
        ),
        out_shape=jax.ShapeDtypeStruct((b, rows, HEAD_DIM), F32),
        compiler_params=_cparams(("arbitrary",), 56),
        name="dsa_sample",
    )(page_table, *args)


def _conv_body(x_ref, st_ref, w_ref, b_ref, o_ref, tail_ref, *, rows):
    @pl.when(pl.program_id(2) == 0)
    def _():
        tail_ref[...] = st_ref[...]

    x = x_ref[...]
    xc = jnp.concatenate([tail_ref[...], x], axis=0)
    out = b_ref[...] + x * w_ref[CONV_W - 1:CONV_W, :]
    for k in range(1, CONV_W):
        shifted = pltpu.roll(xc, k, 0)[SUBLANES:SUBLANES + rows]
        out = out + shifted * w_ref[CONV_W - 1 - k:CONV_W - k, :]
    o_ref[...] = out * _sigmoid(out)
    tail_ref[...] = x[rows - SUBLANES:rows]


def conv_silu(z3, col0, t, state8, conv_w, conv_b, rows, cdim, cb=512):
    b = z3.shape[0]
    assert col0 % cb == 0 and cdim % cb == 0 and t % rows == 0
    return pl.pallas_call(
        functools.partial(_conv_body, rows=rows),
        grid=(b, cdim // cb, t // rows),
        in_specs=[
            pl.BlockSpec((None, rows, cb), lambda i, j, c: (i, c, col0 // cb + j)),
            pl.BlockSpec((None, SUBLANES, cb), lambda i, j, c: (i, 0, j)),
            pl.BlockSpec((CONV_W, cb), lambda i, j, c: (0, j)),
            pl.BlockSpec((1, cb), lambda i, j, c: (0, j)),
        ],
        out_specs=pl.BlockSpec((None, rows, cb), lambda i, j, c: (i, c, j)),
        out_shape=jax.ShapeDtypeStruct((b, t, cdim), F32),
        scratch_shapes=[pltpu.VMEM((SUBLANES, cb), F32)],
        compiler_params=_cparams(("arbitrary", "arbitrary", "arbitrary"), 32),
        name="conv_silu",
    )(z3, state8, conv_w, conv_b.reshape(1, cdim))


def _ssd_pair(xs, dt_raw, dtb, a, dskip, cb, lmask, valid):
    dt = _softplus(dt_raw + dtb)
    if valid is not None:
        dt = jnp.where(valid, dt, 0.0)
    xdt = xs * dt
    acs = jnp.dot(jnp.where(lmask, 1.0, 0.0), dt * a, precision=HI, preferred_element_type=F32)
    acs_t = acs.T
    half = SSM_HEADDIM
    xdt_bf = xdt.astype(BF16)
    yd = []
    for lane0 in (0, half):
        seg = acs[:, lane0:lane0 + 1] - acs_t[lane0:lane0 + 1, :]
        lm = jnp.where(lmask, jnp.exp(jnp.where(lmask, seg, 0.0)), 0.0)
        yd.append(jnp.dot((cb * lm).astype(BF16), xdt_bf, preferred_element_type=F32))
    lane = lax.broadcasted_iota(I32, xs.shape, 1)
    y = jnp.where(lane < half, yd[0], yd[1]) + dskip * xs
    return xdt, acs, y


def _ssd_prompt_body(xs_ref, bm_ref, cm_ref, dt_ref, dtb_ref, a_ref, dsk_ref, y_ref, hf_ref, st_ref,
                     *, n_pairs, n_chunks):
    c = pl.program_id(0)

    @pl.when(c == 0)
    def _():
        st_ref[...] = jnp.zeros_like(st_ref)

    ll = SSD_L
    li = lax.broadcasted_iota(I32, (ll, ll), 0)
    si = lax.broadcasted_iota(I32, (ll, ll), 1)
    lmask = si <= li
    pairs_per_group = n_pairs // SSM_GROUPS
    for g in range(SSM_GROUPS):
        gs = slice(g * D_STATE, (g + 1) * D_STATE)
        bm = bm_ref[:, gs].astype(BF16)
        cm = cm_ref[:, gs].astype(BF16)
        cb = lax.dot_general(cm, bm, NT_DIMS, preferred_element_type=F32)
        for kk in range(pairs_per_group):
            k = g * pairs_per_group + kk
            ks = slice(k * LANES, (k + 1) * LANES)
            xdt, acs, y = _ssd_pair(xs_ref[:, ks], dt_ref[:, ks], dtb_ref[:, ks], a_ref[:, ks],
                                    dsk_ref[:, ks], cb, lmask, None)
            acs_last = acs[ll - 1:ll, :]
            state = st_ref[k]
            y_off = lax.dot_general(cm, state.astype(BF16), NT_DIMS, preferred_element_type=F32)
            y_ref[:, ks] = y + y_off * jnp.exp(acs)
            xd_t = (xdt * jnp.exp(acs_last - acs)).T
            upd = jnp.dot(xd_t.astype(BF16), bm, preferred_element_type=F32)
            cd = jnp.exp(jnp.broadcast_to(acs_last, (ll, LANES))).T[:, 0:1]
            st_ref[k] = state * cd + upd

    @pl.when(c == n_chunks - 1)
    def _():
        hf_ref[...] = st_ref[...]


def ssd_prompt(xc, dt_exp, dtb_exp, a_exp, dsk_exp, d_inner):
    t = xc.shape[0]
    n_pairs = d_inner // LANES
    gn = SSM_GROUPS * D_STATE
    n_chunks = t // SSD_L
    y, hf = pl.pallas_call(
        functools.partial(_ssd_prompt_body, n_pairs=n_pairs, n_chunks=n_chunks),
        grid=(n_chunks,),
        in_specs=[
            pl.BlockSpec((SSD_L, d_inner), lambda c: (c, 0)),
            pl.BlockSpec((SSD_L, gn), lambda c: (c, d_inner // gn)),
            pl.BlockSpec((SSD_L, gn), lambda c: (c, d_inner // gn + 1)),
            pl.BlockSpec((SSD_L, d_inner), lambda c: (c, 0)),
            _const_spec((1, d_inner)),
            _const_spec((1, d_inner)),
            _const_spec((1, d_inner)),
        ],
        out_specs=[
            pl.BlockSpec((SSD_L, d_inner), lambda c: (c, 0)),
            _const_spec((n_pairs, LANES, D_STATE)),
        ],
        out_shape=[
            jax.ShapeDtypeStruct((t, d_inner), F32),
            jax.ShapeDtypeStruct((n_pairs, LANES, D_STATE), F32),
        ],
        scratch_shapes=[pltpu.VMEM((n_pairs, LANES, D_STATE), F32)],
        compiler_params=_cparams(("arbitrary",), 32),
        name="ssd_prompt",
    )(xc, xc, xc, dt_exp, dtb_exp, a_exp, dsk_exp)
    return y, hf.reshape(2 * n_pairs, SSM_HEADDIM, D_STATE)


def _ssd_sample_body(xs_ref, bm_ref, cm_ref, dt_ref, dtb_ref, a_ref, dsk_ref, h0_ref, y_ref, h1_ref,
                     *, n_seq, t_valid):
    ll = n_seq * SEQ_PAD
    li = lax.broadcasted_iota(I32, (ll, ll), 0)
    si = lax.broadcasted_iota(I32, (ll, ll), 1)
    same = lax.shift_right_logical(li, 3) == lax.shift_right_logical(si, 3)
    lmask = same & (si <= li)
    last = same & ((si & (SEQ_PAD - 1)) == SEQ_PAD - 1)
    rowi = lax.broadcasted_iota(I32, (ll, LANES), 0)
    valid = (rowi & (SEQ_PAD - 1)) < t_valid
    bm = bm_ref[...].astype(BF16)
    cm = cm_ref[...].astype(BF16)
    cb = lax.dot_general(cm, bm, NT_DIMS, preferred_element_type=F32)
    xdt, acs, y = _ssd_pair(xs_ref[...], dt_ref[...], dtb_ref[...], a_ref[...], dsk_ref[...],
                            cb, lmask, valid)
    acs_last = jnp.dot(jnp.where(last, 1.0, 0.0), acs, precision=HI, preferred_element_type=F32)
    e_acs = jnp.exp(acs)
    xd_t = (xdt * jnp.exp(acs_last - acs)).T
    cd_t = jnp.exp(acs_last).T
    lane = lax.broadcasted_iota(I32, (LANES, ll), 1)
    cm32 = cm_ref[...]
    y_off = []
    for s in range(n_seq):
        rs = slice(s * SEQ_PAD, (s + 1) * SEQ_PAD)
        state = h0_ref[s].reshape(LANES, D_STATE)
        y_off.append(lax.dot_general(cm32[rs].astype(BF16), state.astype(BF16), NT_DIMS,
                                     preferred_element_type=F32) * e_acs[rs])
        in_seq = (lane >= s * SEQ_PAD) & (lane < (s + 1) * SEQ_PAD)
        upd = jnp.dot(jnp.where(in_seq, xd_t, 0.0).astype(BF16), bm, preferred_element_type=F32)
        new = state * cd_t[:, s * SEQ_PAD:s * SEQ_PAD + 1] + upd
        h1_ref[s] = new.reshape(2, SSM_HEADDIM, D_STATE)
    y_ref[...] = y + jnp.concatenate(y_off, axis=0)


def ssd_sample(xc, dt_exp, dtb_exp, a_exp, dsk_exp, h0, d_inner, t_valid, n_seq=16):
    rows = xc.shape[0]
    b = h0.shape[0]
    n_pairs = d_inner // LANES
    ppg = n_pairs // SSM_GROUPS
    ll = n_seq * SEQ_PAD
    first_b = d_inner // D_STATE
    return pl.pallas_call(
        functools.partial(_ssd_sample_body, n_seq=n_seq, t_valid=t_valid),
        grid=(b // n_seq, n_pairs),
        in_specs=[
            pl.BlockSpec((ll, LANES), lambda s, k: (s, k)),
            pl.BlockSpec((ll, D_STATE), lambda s, k: (s, first_b + k // ppg)),
            pl.BlockSpec((ll, D_STATE), lambda s, k: (s, first_b + SSM_GROUPS + k // ppg)),
            pl.BlockSpec((ll, LANES), lambda s, k: (s, k)),
            pl.BlockSpec((1, LANES), lambda s, k: (0, k)),
            pl.BlockSpec((1, LANES), lambda s, k: (0, k)),
            pl.BlockSpec((1, LANES), lambda s, k: (0, k)),
            pl.BlockSpec((n_seq, 2, SSM_HEADDIM, D_STATE), lambda s, k: (s, k, 0, 0)),
        ],
        out_specs=[
            pl.BlockSpec((ll, LANES), lambda s, k: (s, k)),
            pl.BlockSpec((n_seq, 2, SSM_HEADDIM, D_STATE), lambda s, k: (s, k, 0, 0)),
        ],
        out_shape=[
            jax.ShapeDtypeStruct((rows, d_inner), F32),
            jax.ShapeDtypeStruct(h0.shape, F32),
        ],
        compiler_params=_cparams(("arbitrary", "arbitrary"), 32),
        name="ssd_sample",
    )(xc, xc, xc, dt_exp, dtb_exp, a_exp, dsk_exp, h0)


def _gated_norm_body(y_ref, z_ref, g_ref, o_ref, *, d_inner):
    z = z_ref[...]
    yg = y_ref[...] * (z * _sigmoid(z))
    gw = d_inner // SSM_GROUPS
    for g in range(SSM_GROUPS):
        gs = slice(g * gw, (g + 1) * gw)
        v = yg[:, gs]
        r = lax.rsqrt(jnp.mean(v * v, axis=-1, keepdims=True) + EPS)
        o_ref[:, gs] = (v * r * g_ref[:, gs]).astype(BF16)


def gated_norm(y, z, norm_g, d_inner, tm=TM):
    t = y.shape[0]
    return pl.pallas_call(
        functools.partial(_gated_norm_body, d_inner=d_inner),
        grid=(t // tm,),
        in_specs=[
            pl.BlockSpec((tm, d_inner), lambda i: (i, 0)),
            pl.BlockSpec((tm, d_inner), lambda i: (i, 0)),
            _const_spec((1, d_inner)),
        ],
        out_specs=pl.BlockSpec((tm, d_inner), lambda i: (i, 0)),
        out_shape=jax.ShapeDtypeStruct((t, d_inner), BF16),
        compiler_params=_cparams(("arbitrary",), 32),
        name="gated_norm",
    )(y, z, norm_g.reshape(1, d_inner))


def _pack_cols(w, pieces, total):
    cols = [w[:, a:b] for a, b in pieces]
    used = sum(b - a for a, b in pieces)
    if total > used:
        cols.append(jnp.zeros((w.shape[0], total - used), w.dtype))
    return jnp.concatenate(cols, axis=1).astype(BF16)


def _pad_rows(a, rows, axis):
    pad = [(0, 0)] * a.ndim
    pad[axis] = (0, rows - a.shape[axis])
    return jnp.pad(a, pad)


def _dsa_layer(x_all, n_prompt, n_seq, t_s, mix_g, w_in, rel_bias, cache_k, cache_v, cache_kidx, layer,
               page_table):
    qw, kvw, qiw = N_HEADS_A * HEAD_DIM, KV_HEADS * HEAD_DIM, IDX_HEADS * IDX_DIM
    o_q, o_k, o_v, o_qi = 0, qw, qw + kvw, qw + 2 * kvw
    o_ki = o_qi + qiw
    o_wi = o_ki + IDX_DIM
    o_qm = o_wi + IDX_HEADS
    col = {"q": 0, "k": qw, "qi": qw + kvw, "v": qw + kvw + qiw, "qm": qw + 2 * kvw + qiw}
    col["kw"] = col["qm"] + MEM_WIDTH
    width = col["kw"] + LANES
    w = _pack_cols(w_in, [(o_q, o_q + qw), (o_k, o_k + kvw), (o_qi, o_qi + qiw), (o_v, o_v + kvw),
                          (o_qm, o_qm + MEM_WIDTH), (o_ki, o_ki + IDX_DIM), (o_wi, o_wi + IDX_HEADS)], width)
    z = norm_matmul(x_all, mix_g, w, TM, width // 3)

    k_all = z[:, col["k"]:col["k"] + kvw]
    v_all = z[:, col["v"]:col["v"] + kvw]
    ki_all = z[:, col["kw"]:col["kw"] + IDX_DIM]

    kiw_bf = z[:n_prompt, col["kw"]:col["kw"] + LANES].astype(BF16)
    mix_p = dsa_prompt(z, n_prompt, col, kiw_bf, k_all[:n_prompt].astype(BF16),
                       v_all[:n_prompt].astype(BF16), rel_bias)

    zs = z[n_prompt:].reshape(n_seq, t_s, width)
    zs8 = _pad_rows(zs, SEQ_PAD, 1)
    qi8 = zs8[:, :, col["qi"]:col["qi"] + qiw].reshape(n_seq, SEQ_PAD * IDX_HEADS, IDX_DIM)
    w8 = zs8[:, :, col["kw"] + IDX_DIM:col["kw"] + IDX_DIM + IDX_HEADS].reshape(n_seq, 1, SEQ_PAD * IDX_HEADS)
    q8 = zs8[:, :, :qw].reshape(n_seq, SEQ_PAD, N_HEADS_A, HEAD_DIM).transpose(0, 2, 1, 3)
    q8 = q8.reshape(n_seq, N_HEADS_A * SEQ_PAD, HEAD_DIM)
    kidx_new = _pad_rows(zs[:, :, col["kw"]:col["kw"] + IDX_DIM], PAGE_SIZE, 1)
    k_new = _pad_rows(zs[:, :, col["k"]:col["k"] + kvw], PAGE_SIZE, 1)
    v_new = _pad_rows(zs[:, :, col["v"]:col["v"] + kvw], PAGE_SIZE, 1)
    o_s = dsa_sample(qi8, w8, q8, kidx_new, k_new, v_new, cache_kidx, cache_k, cache_v, layer,
                     page_table, rel_bias, t_s)
    mix_s = o_s.reshape(n_seq, N_HEADS_A, SEQ_PAD, HEAD_DIM)[:, :, :t_s].transpose(0, 2, 1, 3)
    mix_s = mix_s.reshape(n_seq * t_s, qw).astype(BF16)
    return z, col["qm"] // MEM_WIDTH, jnp.concatenate([mix_p, mix_s], axis=0), k_all, v_all, ki_all


def _ssd_layer(x_all, n_prompt, n_seq, t_s, mix_g, w_in, conv_w, conv_b, dt_bias, a_log, d_skip, norm_g,
               state_conv, state_ssm):
    n_heads = dt_bias.shape[0]
    d_inner = n_heads * SSM_HEADDIM
    cdim = d_inner + 2 * SSM_GROUPS * D_STATE
    o_x, o_dt = d_inner, d_inner + cdim
    o_qm = o_dt + n_heads
    c_qm = d_inner + cdim
    c_dt = c_qm + MEM_WIDTH
    width = _round_up(c_dt + n_heads, 2 * LANES)
    w = _pack_cols(w_in, [(0, d_inner), (o_x, o_x + cdim), (o_qm, o_qm + MEM_WIDTH), (o_dt, o_dt + n_heads)], width)
    z = norm_matmul(x_all, mix_g, w, TM, width // 2)
    xbc = z[:, d_inner:d_inner + cdim]
    dt_exp = jnp.repeat(z[:, c_dt:c_dt + n_heads], SSM_HEADDIM, axis=1)
    dtb_exp = jnp.repeat(dt_bias.astype(F32), SSM_HEADDIM).reshape(1, d_inner)
    a_exp = -jnp.exp(jnp.repeat(a_log.astype(F32), SSM_HEADDIM)).reshape(1, d_inner)
    dsk_exp = jnp.repeat(d_skip.astype(F32), SSM_HEADDIM).reshape(1, d_inner)

    xc_p = conv_silu(z[None], d_inner, n_prompt, jnp.zeros((1, SUBLANES, cdim), F32), conv_w, conv_b,
                     8 * SSD_L, cdim)[0]
    y_p, hf_p = ssd_prompt(xc_p, dt_exp, dtb_exp, a_exp, dsk_exp, d_inner)
    new_conv_p = xbc[n_prompt - (CONV_W - 1):n_prompt]

    xbc_s = xbc[n_prompt:].reshape(n_seq, t_s, cdim)
    xbc_s8 = _pad_rows(xbc_s, SEQ_PAD, 1)
    st8 = jnp.concatenate([jnp.zeros((n_seq, SUBLANES - (CONV_W - 1), cdim), F32), state_conv.astype(F32)], axis=1)
    xc_s = conv_silu(xbc_s8, 0, SEQ_PAD, st8, conv_w, conv_b, SEQ_PAD, cdim, cb=cdim)
    xc_s = xc_s.reshape(n_seq * SEQ_PAD, cdim)
    dt_s8 = _pad_rows(dt_exp[n_prompt:].reshape(n_seq, t_s, d_inner), SEQ_PAD, 1).reshape(n_seq * SEQ_PAD, d_inner)
    y_s8, hf_s = ssd_sample(xc_s, dt_s8, dtb_exp, a_exp, dsk_exp, state_ssm.astype(F32), d_inner, t_s)
    y_s = y_s8.reshape(n_seq, SEQ_PAD, d_inner)[:, :t_s].reshape(n_seq * t_s, d_inner)
    new_conv_s = jnp.concatenate([state_conv.astype(F32), xbc_s], axis=1)[:, -(CONV_W - 1):]

    mix = gated_norm(jnp.concatenate([y_p, y_s], axis=0), z, norm_g, d_inner)
    return z, c_qm // MEM_WIDTH, mix, hf_p, new_conv_p, hf_s, new_conv_s


def kernel(x_prompt, x_sample, mem_prompt, cache_k, cache_v, cache_kidx, page_table, state_ssm, state_conv,
           cache_mem_k, cache_mem_v, rel_bias, ffn1_g, ffn1_w_gu, ffn1_w_down, mix_g, mem_g, w_mem_kv,
           w_in_attn, w_in_ssd, conv_w, conv_b, dt_bias, a_log, d_skip, ssd_norm_g, w_out,
           ffn2_g, ffn2_w_gu, ffn2_w_down, final_g):
    bp, n_prompt, d = x_prompt.shape
    n_seq, t_s, _ = x_sample.shape
    assert bp == 1
    depth = ffn1_g.shape[0]
    n_mem = mem_prompt.shape[1]
    x_all = jnp.concatenate([x_prompt[0], x_sample.reshape(n_seq * t_s, d)], axis=0)
    outs = {k: [] for k in ("pk", "pv", "pki", "pssm", "pconv", "pmk", "pmv", "sk", "sv", "ski", "sssm", "sconv")}
    y_all = None
    ffn1_w = prep_ffn_weights(ffn1_w_gu, ffn1_w_down, TF)
    ffn2_w = prep_ffn_weights(ffn2_w_gu, ffn2_w_down, TF)
    for i in range(depth):
        j = i // 2
        x_all = ffn(x_all, ffn1_g[i], ffn1_w, i)
        mkv = norm_matmul(mem_prompt[0], mem_g[i], w_mem_kv[i].astype(BF16), n_mem, 2 * MEM_WIDTH)
        outs["pmk"].append(mkv[:, :MEM_WIDTH].reshape(1, n_mem, MEM_HEADS, MEM_HEAD_DIM))
        outs["pmv"].append(mkv[:, MEM_WIDTH:].reshape(1, n_mem, MEM_HEADS, MEM_HEAD_DIM))
        if i % 2 == 0:
            z, qm_blk, mix, k_all, v_all, ki_all = _dsa_layer(
                x_all, n_prompt, n_seq, t_s, mix_g[i], w_in_attn[j], rel_bias,
                cache_k, cache_v, cache_kidx, j, page_table)
            outs["pk"].append(k_all[:n_prompt].reshape(1, n_prompt, KV_HEADS, HEAD_DIM))
            outs["pv"].append(v_all[:n_prompt].reshape(1, n_prompt, KV_HEADS, HEAD_DIM))
            outs["pki"].append(ki_all[:n_prompt].reshape(1, n_prompt, IDX_DIM))
            outs["sk"].append(k_all[n_prompt:].reshape(n_seq, t_s, KV_HEADS, HEAD_DIM))
            outs["sv"].append(v_all[n_prompt:].reshape(n_seq, t_s, KV_HEADS, HEAD_DIM))
            outs["ski"].append(ki_all[n_prompt:].reshape(n_seq, t_s, IDX_DIM))
        else:
            z, qm_blk, mix, hf_p, conv_p, hf_s, conv_s = _ssd_layer(
                x_all, n_prompt, n_seq, t_s, mix_g[i], w_in_ssd[j], conv_w[j], conv_b[j], dt_bias[j],
                a_log[j], d_skip[j], ssd_norm_g[j], state_conv[j], state_ssm[j])
            outs["pssm"].append(hf_p[None])
            outs["pconv"].append(conv_p[None])
            outs["sssm"].append(hf_s)
            outs["sconv"].append(conv_s)
        mem_p = mem_attn_prompt(z, qm_blk, n_prompt, mkv)
        qm_s = z[n_prompt:, qm_blk * MEM_WIDTH:(qm_blk + 1) * MEM_WIDTH].reshape(n_seq, t_s, MEM_WIDTH)
        mem_s = mem_attn_sample(_pad_rows(qm_s, SEQ_PAD, 1), cache_mem_k, cache_mem_v, i)
        mem = jnp.concatenate([mem_p, mem_s[:, :t_s].reshape(n_seq * t_s, MEM_WIDTH)], axis=0)
        x_all = out_proj(x_all, mix, mem, w_out[i])
        if i == depth - 1:
            x_all, y_all = ffn(x_all, ffn2_g[i], ffn2_w, i, final_g=final_g)
        else:
            x_all = ffn(x_all, ffn2_g[i], ffn2_w, i)
    y_prompt = y_all[:n_prompt][None]
    y_sample = y_all[n_prompt:].reshape(n_seq, t_s, d)
    st = lambda k: jnp.stack(outs[k])
    return (y_prompt, y_sample, st("pk"), st("pv"), st("pki"), st("pssm"), st("pconv"), st("pmk"), st("pmv"),
            st("sk"), st("sv"), st("ski"), st("sssm"), st("sconv"))
```

```python
import functools
import math

import numpy as np
import jax
import jax.numpy as jnp
from jax import lax
from jax.experimental import pallas as pl
from jax.experimental.pallas import tpu as pltpu

F32, BF16, I32 = jnp.float32, jnp.bfloat16, jnp.int32
HI = lax.Precision.HIGHEST
NT_DIMS = (((1,), (1,)), ((), ()))

HEAD_DIM = 128
KV_HEADS = 4
Q_PER_KV = 3
N_HEADS_A = KV_HEADS * Q_PER_KV
IDX_HEADS = 16
IDX_DIM = 64
TOPK_MAX = 256
REL_BUCKETS = 32
REL_MAX_EXACT = 16
REL_MAX_DIST = 128
MEM_HEADS = 4
MEM_HEAD_DIM = 128
MEM_WIDTH = MEM_HEADS * MEM_HEAD_DIM
SSM_HEADDIM = 64
SSM_GROUPS = 4
D_STATE = 128
CONV_W = 4
PAGE_SIZE = 128
EPS = 1e-6

LANES = 128
SUBLANES = 8
NEG = -1e30
INT_MIN = -2 ** 31
IDX_SCALE = IDX_DIM ** -0.5 * IDX_HEADS ** -0.5
ATT_SCALE = HEAD_DIM ** -0.5
MEM_SCALE = MEM_HEAD_DIM ** -0.5

QB = 128
KC = 512
TM = 512
TF = 512
SSD_L = 128
SEQ_PAD = 8


def _cparams(sem, vmem_mb):
    return pltpu.CompilerParams(dimension_semantics=sem, vmem_limit_bytes=vmem_mb * 2 ** 20)


def _round_up(n, m):
    return (n + m - 1) // m * m


def _const_spec(shape):
    nd = len(shape)
    return pl.BlockSpec(shape, lambda *_: (0,) * nd)


def _resident_spec(shape):
    nd = len(shape)
    return pl.BlockSpec(shape, lambda *_: (0,) * nd, pipeline_mode=pl.Buffered(1))


def _rms(x, g):
    return x * lax.rsqrt(jnp.mean(x * x, axis=-1, keepdims=True) + EPS) * g


def _sigmoid(x):
    return 1.0 / (1.0 + jnp.exp(-x))


def _softplus(x):
    return jnp.maximum(x, 0.0) + jnp.log1p(jnp.exp(-jnp.abs(x)))


def _ffn_body(x_ref, g_ref, wg_ref, wu_ref, wd_ref, *rest, nj, with_final):
    if with_final:
        fg_ref, o_ref, y_ref, xn_ref, acc_ref = rest
    else:
        o_ref, xn_ref, acc_ref = rest
    j = pl.program_id(1)

    @pl.when(j == 0)
    def _():
        xn_ref[...] = _rms(x_ref[...], g_ref[...]).astype(BF16)
        acc_ref[...] = jnp.zeros_like(acc_ref)

    xn = xn_ref[...]
    gate = jnp.dot(xn, wg_ref[...], preferred_element_type=F32)
    up = jnp.dot(xn, wu_ref[...], preferred_element_type=F32)
    a = gate * _sigmoid(gate) * up
    acc_ref[...] += jnp.dot(a.astype(BF16), wd_ref[...], preferred_element_type=F32)

    @pl.when(j == nj - 1)
    def _():
        o = x_ref[...] + 0.5 * acc_ref[...]
        o_ref[...] = o
        if with_final:
            y_ref[...] = _rms(o, fg_ref[...])


def _cast_gate_up_body(g_ref, u_ref, og_ref, ou_ref, *, n_valid):
    keep = pl.program_id(1) < n_valid
    og_ref[...] = jnp.where(keep, g_ref[...], 0.0).astype(BF16)
    ou_ref[...] = jnp.where(keep, u_ref[...], 0.0).astype(BF16)


def _cast_down_body(w_ref, o_ref, *, n_valid):
    o_ref[...] = jnp.where(pl.program_id(1) < n_valid, w_ref[...], 0.0).astype(BF16)


def prep_ffn_weights(w_gu, w_d, tf):
    nl, d, two_ff = w_gu.shape
    ff = two_ff // 2
    ffp = _round_up(ff, tf)
    assert ff % LANES == 0
    nb, nbp = ff // LANES, ffp // LANES
    wg, wu = pl.pallas_call(
        functools.partial(_cast_gate_up_body, n_valid=nb),
        grid=(nl, nbp),
        in_specs=[pl.BlockSpec((None, d, LANES), lambda l, j: (l, 0, jnp.minimum(j, nb - 1))),
                  pl.BlockSpec((None, d, LANES), lambda l, j: (l, 0, nb + jnp.minimum(j, nb - 1)))],
        out_specs=[pl.BlockSpec((None, d, LANES), lambda l, j: (l, 0, j))] * 2,
        out_shape=[jax.ShapeDtypeStruct((nl, d, ffp), BF16)] * 2,
        compiler_params=_cparams(("arbitrary", "arbitrary"), 32),
        name="cast_gate_up",
    )(w_gu, w_gu)
    wd = pl.pallas_call(
        functools.partial(_cast_down_body, n_valid=nb),
        grid=(nl, nbp),
        in_specs=[pl.BlockSpec((None, LANES, d), lambda l, j: (l, jnp.minimum(j, nb - 1), 0))],
        out_specs=pl.BlockSpec((None, LANES, d), lambda l, j: (l, j, 0)),
        out_shape=jax.ShapeDtypeStruct((nl, ffp, d), BF16),
        compiler_params=_cparams(("arbitrary", "arbitrary"), 32),
        name="cast_down",
    )(w_d)
    return wg, wu, wd


def ffn(x, g, weights, layer, final_g=None, tm=TM, tf=TF):
    t, d = x.shape
    wg, wu, wd = weights
    nj = wg.shape[2] // tf
    with_final = final_g is not None
    in_specs = [
        pl.BlockSpec((tm, d), lambda i, j: (i, 0)),
        _const_spec((1, d)),
        pl.BlockSpec((None, d, tf), lambda i, j: (layer, 0, j)),
        pl.BlockSpec((None, d, tf), lambda i, j: (layer, 0, j)),
        pl.BlockSpec((None, tf, d), lambda i, j: (layer, j, 0)),
    ]
    args = [x, g.reshape(1, d), wg, wu, wd]
    out_shape = [jax.ShapeDtypeStruct((t, d), F32)]
    out_specs = [pl.BlockSpec((tm, d), lambda i, j: (i, 0))]
    if with_final:
        in_specs.append(_const_spec((1, d)))
        args.append(final_g.reshape(1, d))
        out_shape.append(jax.ShapeDtypeStruct((t, d), F32))
        out_specs.append(pl.BlockSpec((tm, d), lambda i, j: (i, 0)))
    res = pl.pallas_call(
        functools.partial(_ffn_body, nj=nj, with_final=with_final),
        grid=(t // tm, nj),
        in_specs=in_specs,
        out_specs=out_specs,
        out_shape=out_shape,
        scratch_shapes=[pltpu.VMEM((tm, d), BF16), pltpu.VMEM((tm, d), F32)],
        compiler_params=_cparams(("arbitrary", "arbitrary"), 56),
        name="ffn",
    )(*args)
    return res if with_final else res[0]


def _norm_matmul_body(x_ref, g_ref, w_ref, o_ref, xn_ref):
    @pl.when(pl.program_id(1) == 0)
    def _():
        xn_ref[...] = _rms(x_ref[...], g_ref[...]).astype(BF16)

    o_ref[...] = jnp.dot(xn_ref[...], w_ref[...], preferred_element_type=F32)


def norm_matmul(x, g, w_bf16, tm, tn):
    t, d = x.shape
    n = w_bf16.shape[1]
    return pl.pallas_call(
        _norm_matmul_body,
        grid=(t // tm, n // tn),
        in_specs=[
            pl.BlockSpec((tm, d), lambda i, j: (i, 0)),
            _const_spec((1, d)),
            pl.BlockSpec((d, tn), lambda i, j: (0, j)),
        ],
        out_specs=pl.BlockSpec((tm, tn), lambda i, j: (i, j)),
        out_shape=jax.ShapeDtypeStruct((t, n), F32),
        scratch_shapes=[pltpu.VMEM((tm, d), BF16)],
        compiler_params=_cparams(("arbitrary", "arbitrary"), 48),
        name="norm_matmul",
    )(x, g.reshape(1, d), w_bf16)


def _out_proj_body(x_ref, mix_ref, mem_ref, w1_ref, w2_ref, o_ref):
    o_ref[...] = (x_ref[...]
                  + jnp.dot(mix_ref[...], w1_ref[...], preferred_element_type=F32)
                  + jnp.dot(mem_ref[...], w2_ref[...], preferred_element_type=F32))


def out_proj(x, mix, mem, w_out, tm=TM):
    t, d = x.shape
    dm, dw = mix.shape[1], mem.shape[1]
    w1 = w_out[:dm].astype(BF16)
    w2 = w_out[dm:].astype(BF16)
    return pl.pallas_call(
        _out_proj_body,
        grid=(t // tm,),
        in_specs=[
            pl.BlockSpec((tm, d), lambda i: (i, 0)),
            pl.BlockSpec((tm, dm), lambda i: (i, 0)),
            pl.BlockSpec((tm, dw), lambda i: (i, 0)),
            _const_spec((dm, d)),
            _const_spec((dw, d)),
        ],
        out_specs=pl.BlockSpec((tm, d), lambda i: (i, 0)),
        out_shape=jax.ShapeDtypeStruct((t, d), F32),
        compiler_params=_cparams(("arbitrary",), 48),
        name="out_proj",
    )(x, mix, mem, w1, w2)


def _softmax_rows(s):
    m = jnp.max(s, axis=-1, keepdims=True)
    p = jnp.exp(s - m)
    return p, jnp.sum(p, axis=-1, keepdims=True)


def _mem_attn_prompt_body(q_ref, mk_ref, mv_ref, o_ref):
    for h in range(MEM_HEADS):
        sl = slice(h * MEM_HEAD_DIM, (h + 1) * MEM_HEAD_DIM)
        q = (q_ref[:, sl] * MEM_SCALE).astype(BF16)
        s = lax.dot_general(q, mk_ref[:, sl].astype(BF16), NT_DIMS, preferred_element_type=F32)
        p, l = _softmax_rows(s)
        o = jnp.dot(p.astype(BF16), mv_ref[:, sl].astype(BF16), preferred_element_type=F32)
        o_ref[:, sl] = (o / l).astype(BF16)


def mem_attn_prompt(z, qm_col_block, n_rows, mkv, tm=TM):
    m = mkv.shape[0]
    return pl.pallas_call(
        _mem_attn_prompt_body,
        grid=(n_rows // tm,),
        in_specs=[
            pl.BlockSpec((tm, MEM_WIDTH), lambda i: (i, qm_col_block)),
            pl.BlockSpec((m, MEM_WIDTH), lambda i: (0, 0)),
            pl.BlockSpec((m, MEM_WIDTH), lambda i: (0, 1)),
        ],
        out_specs=pl.BlockSpec((tm, MEM_WIDTH), lambda i: (i, 0)),
        out_shape=jax.ShapeDtypeStruct((n_rows, MEM_WIDTH), BF16),
        compiler_params=_cparams(("arbitrary",), 32),
        name="mem_attn_prompt",
    )(z, mkv, mkv)


def _mem_attn_sample_body(q_ref, k_ref, v_ref, o_ref, *, bb, m):
    rows, cols = MEM_HEADS * SEQ_PAD, m * MEM_HEADS
    row_head = lax.shift_right_logical(lax.broadcasted_iota(I32, (rows, cols), 0), SEQ_PAD.bit_length() - 1)
    col_head = lax.broadcasted_iota(I32, (rows, cols), 1) & (MEM_HEADS - 1)
    madd = jnp.where(row_head == col_head, 0.0, NEG)
    for b in range(bb):
        q = jnp.concatenate([q_ref[b, :, h * MEM_HEAD_DIM:(h + 1) * MEM_HEAD_DIM] for h in range(MEM_HEADS)], axis=0)
        s = lax.dot_general((q * MEM_SCALE).astype(BF16), k_ref[b].astype(BF16), NT_DIMS,
                            preferred_element_type=F32)
        p, l = _softmax_rows(s + madd)
        o = jnp.dot(p.astype(BF16), v_ref[b].astype(BF16), preferred_element_type=F32) / l
        for h in range(MEM_HEADS):
            o_ref[b, :, h * MEM_HEAD_DIM:(h + 1) * MEM_HEAD_DIM] = o[h * SEQ_PAD:(h + 1) * SEQ_PAD].astype(BF16)


def mem_attn_sample(qm8, mem_k, mem_v, layer, bb=8):
    depth, b, m, nh, hd = mem_k.shape
    mem_k = mem_k.reshape(depth, b, m * nh, hd)
    mem_v = mem_v.reshape(depth, b, m * nh, hd)
    cache_spec = pl.BlockSpec((None, bb, m * nh, hd), lambda i: (layer, i, 0, 0))
    return pl.pallas_call(
        functools.partial(_mem_attn_sample_body, bb=bb, m=m),
        grid=(b // bb,),
        in_specs=[pl.BlockSpec((bb, SEQ_PAD, MEM_WIDTH), lambda i: (i, 0, 0)), cache_spec, cache_spec],
        out_specs=pl.BlockSpec((bb, SEQ_PAD, MEM_WIDTH), lambda i: (i, 0, 0)),
        out_shape=jax.ShapeDtypeStruct((b, SEQ_PAD, MEM_WIDTH), BF16),
        compiler_params=_cparams(("arbitrary",), 40),
        name="mem_attn_sample",
    )(qm8, mem_k, mem_v)


def _sortable_key(x):
    b = pltpu.bitcast(x, I32)
    return jnp.where(b < 0, (b ^ 0x7FFFFFFF) + 1, b)


def _t5_bucket_np(dist):
    n = np.maximum(dist, 0)
    nf = np.maximum(n, 1).astype(np.float64)
    large = REL_MAX_EXACT + (np.log(nf / REL_MAX_EXACT) / math.log(REL_MAX_DIST / REL_MAX_EXACT)
                             * (REL_BUCKETS - REL_MAX_EXACT)).astype(np.int32)
    large = np.minimum(large, REL_BUCKETS - 1)
    return np.where(n < REL_MAX_EXACT, n, large)


FAR_DIST = int(np.min(np.nonzero(_t5_bucket_np(np.arange(4 * REL_MAX_DIST)) == REL_BUCKETS - 1)[0]))
assert np.all(_t5_bucket_np(np.arange(FAR_DIST, 1 << 16)) == REL_BUCKETS - 1) and FAR_DIST <= QB


def _toeplitz_bias(rel_bias, n, m, k0):
    p = n + m
    d = k0 + (n - 1) - np.arange(p)
    b = _t5_bucket_np(d)
    keep = (d >= 0) & (b != REL_BUCKETS - 1)
    u = jnp.where(keep[:, None], rel_bias[b] - rel_bias[REL_BUCKETS - 1], 0.0).T
    skew = jnp.tile(u, (1, n))[:, :n * (p - 1)].reshape(-1, n, p - 1)
    return skew[:, :, n - 1:n - 1 + m].astype(F32)


def _threshold_search(count_ge, count_tie_lt, shape, topk, n_idx_bits, all_idx):
    zero = jnp.zeros(shape, I32)
    c0 = count_ge(zero)
    t = jnp.where(c0 >= topk, zero, jnp.full(shape, INT_MIN, I32))
    n_ge = jnp.where(c0 >= topk, c0, -NEG)

    def bit_body(b, carry):
        t, n_ge = carry
        cand = t | lax.shift_left(jnp.int32(1), 30 - b)
        c = count_ge(cand)
        ok = c >= topk
        return jnp.where(ok, cand, t), jnp.where(ok, c, n_ge)

    t, n_ge = lax.fori_loop(0, 31, bit_body, (t, n_ge))
    has_k = t > INT_MIN
    excess = jnp.max(jnp.where(has_k, n_ge, 0.0)) > topk

    def tie_search(_):
        need = topk - count_ge(t + 1)

        def jbit(b, j):
            cand = j | lax.shift_left(jnp.int32(1), n_idx_bits - 1 - b)
            return jnp.where(count_tie_lt(t, cand) < need, cand, j)
        return lax.fori_loop(0, n_idx_bits, jbit, jnp.zeros(shape, I32))

    j = lax.cond(excess, tie_search, lambda _: jnp.full(shape, all_idx, I32), 0)
    j = jnp.where(has_k, j, -1)
    return t, j


def _dsa_prompt_body(q_ref, qi_ref, kw_ref, kiw_ref, k_ref, v_ref, bias_ref, o_ref,
                     rhs_ref, wt_ref, key_ref, qe_ref, m_ref, acc_ref, p_ref, s_ref, *, seq, topk):
    i = pl.program_id(0)
    q0 = i * QB

    qi_t = qi_ref[...].T
    pad = jnp.zeros((LANES - IDX_DIM, QB), F32)
    for h in range(IDX_HEADS):
        blk = jnp.concatenate([qi_t[h * IDX_DIM:(h + 1) * IDX_DIM], pad], axis=0)
        rhs_ref[:, h * QB:(h + 1) * QB] = blk.astype(BF16)
    wt_ref[...] = kw_ref[...].T * IDX_SCALE

    n_chunks = (jnp.maximum(i + 1, 2) * QB + KC - 1) // KC

    def score_chunk(c, carry):
        c0 = pl.multiple_of(c * KC, KC)
        x = jnp.dot(kiw_ref[pl.ds(c0, KC), :], rhs_ref[...], preferred_element_type=F32)
        acc = jnp.zeros((KC, QB), F32)
        for h in range(IDX_HEADS):
            acc = acc + jnp.maximum(x[:, h * QB:(h + 1) * QB], 0.0) * wt_ref[IDX_DIM + h:IDX_DIM + h + 1, :]
        kpos = c0 + lax.broadcasted_iota(I32, (KC, QB), 0)
        qpos = q0 + lax.broadcasted_iota(I32, (KC, QB), 1)
        key_ref[pl.ds(c0, KC), :] = jnp.where(kpos <= qpos, _sortable_key(acc), INT_MIN)
        return carry

    lax.fori_loop(0, n_chunks, score_chunk, 0)

    def column_count(hit_of_chunk):
        acc_rows = 8 * SUBLANES

        def body(c, a):
            c0 = pl.multiple_of(c * KC, KC)
            hit = jnp.where(hit_of_chunk(key_ref[pl.ds(c0, KC), :], c0), 1.0, 0.0)
            return a + hit.reshape(KC // acc_rows, acc_rows, QB).sum(axis=0)

        a = lax.fori_loop(0, n_chunks, body, jnp.zeros((acc_rows, QB), F32))
        return jnp.sum(a, axis=0, keepdims=True)

    def count_ge(cand):
        return column_count(lambda kk, c0: kk >= cand)

    def count_tie_lt(t, jc):
        row = lax.broadcasted_iota(I32, (KC, QB), 0)
        return column_count(lambda kk, c0: (kk == t) & (c0 + row < jc))

    t, j = _threshold_search(count_ge, count_tie_lt, (1, QB), topk, (seq - 1).bit_length(), seq)

    eye = jnp.where(lax.broadcasted_iota(I32, (QB, QB), 0) == lax.broadcasted_iota(I32, (QB, QB), 1),
                    1.0, 0.0).astype(BF16)
    for g in range(KV_HEADS):
        for r in range(Q_PER_KV):
            hq = g * Q_PER_KV + r
            rs = slice(r * QB, (r + 1) * QB)
            qe_ref[g, rs, 0:HEAD_DIM] = (q_ref[:, hq * HEAD_DIM:(hq + 1) * HEAD_DIM] * ATT_SCALE).astype(BF16)
            qe_ref[g, rs, HEAD_DIM:2 * HEAD_DIM] = eye
    m_ref[...] = jnp.full(m_ref.shape, 3 * NEG, F32)
    acc_ref[...] = jnp.zeros(acc_ref.shape, F32)

    def mask_t(c0, width, far_end):
        kk = key_ref[pl.ds(c0, width), :]
        kpos = c0 + lax.broadcasted_iota(I32, (width, QB), 0)
        sel = (kk > t) | ((kk == t) & (kpos <= j))
        if far_end is not None:
            sel = sel & (kpos < far_end)
        return jnp.where(sel, 0.0, NEG).astype(BF16)

    def add_pv(g, p, c0, width):
        gs = slice(g * HEAD_DIM, (g + 1) * HEAD_DIM)
        v_ext = jnp.concatenate([v_ref[pl.ds(c0, width), gs], jnp.ones((width, HEAD_DIM), BF16)], axis=1)
        acc_ref[g] = acc_ref[g] + jnp.dot(p, v_ext, preferred_element_type=F32)

    def logits(g, c0, width, madd_t):
        gs = slice(g * HEAD_DIM, (g + 1) * HEAD_DIM)
        k_ext = jnp.concatenate([k_ref[pl.ds(c0, width), gs], madd_t], axis=1)
        return lax.dot_general(qe_ref[g], k_ext, NT_DIMS, preferred_element_type=F32)

    def probs(g, s):
        m_old = m_ref[g]
        m_new = jnp.maximum(m_old, jnp.max(s, axis=1, keepdims=True))
        alpha = jnp.exp(m_old - m_new)
        acc_ref[g] = jnp.concatenate([alpha, alpha], axis=1) * acc_ref[g]
        m_ref[g] = m_new
        return jnp.exp((s - m_new[:, 0:1]).astype(BF16))

    far_end = jnp.maximum(i - 1, 0) * QB
    n_far = (far_end + KC - 1) // KC
    last0 = pl.multiple_of(jnp.maximum(n_far - 1, 0) * KC, KC)
    p_ref[...] = jnp.zeros(p_ref.shape, BF16)
    madd0 = mask_t(0, KC, far_end)
    for g in range(KV_HEADS):
        s_ref[0, g] = logits(g, 0, KC, madd0)

    def far_body(c, carry):
        cur = c & 1
        nxt = 1 - cur
        prev0 = pl.multiple_of(jnp.maximum(c - 1, 0) * KC, KC)
        next0 = pl.multiple_of(jnp.minimum((c + 1) * KC, last0), KC)
        madd_next = mask_t(next0, KC, far_end)
        for g in range(KV_HEADS):
            add_pv(g, p_ref[nxt, g], prev0, KC)
            s_ref[nxt, g] = logits(g, next0, KC, madd_next)
            p_ref[cur, g] = probs(g, s_ref[cur, g])
        return carry

    lax.fori_loop(0, n_far, far_body, 0)
    last_buf = jnp.maximum(n_far - 1, 0) & 1
    near0 = pl.multiple_of(far_end, QB)
    variant = jnp.minimum(i, 1)
    madd_near = mask_t(near0, 2 * QB, None)
    for g in range(KV_HEADS):
        add_pv(g, p_ref[last_buf, g], last0, KC)
        s = logits(g, near0, 2 * QB, madd_near) + bias_ref[variant, g]
        add_pv(g, probs(g, s), near0, 2 * QB)

    for g in range(KV_HEADS):
        o = acc_ref[g, :, 0:HEAD_DIM] / acc_ref[g, :, HEAD_DIM:2 * HEAD_DIM]
        for r in range(Q_PER_KV):
            hq = g * Q_PER_KV + r
            o_ref[:, hq * HEAD_DIM:(hq + 1) * HEAD_DIM] = o[r * QB:(r + 1) * QB].astype(BF16)


def _prompt_bias(rel_bias):
    out = [_toeplitz_bias(rel_bias, QB, 2 * QB, k0).reshape(KV_HEADS, Q_PER_KV * QB, 2 * QB) for k0 in (0, QB)]
    return jnp.stack(out)


def dsa_prompt(z, seq, col, kiw_bf, k_bf, v_bf, rel_bias):
    topk = min(TOPK_MAX, seq // 4)
    assert seq % KC == 0 and seq >= 2 * QB
    qw = N_HEADS_A * HEAD_DIM
    qiw = IDX_HEADS * IDX_DIM
    bias = _prompt_bias(rel_bias)
    rows = Q_PER_KV * QB
    return pl.pallas_call(
        functools.partial(_dsa_prompt_body, seq=seq, topk=topk),
        grid=(seq // QB,),
        in_specs=[
            pl.BlockSpec((QB, qw), lambda i: (i, col["q"] // qw)),
            pl.BlockSpec((QB, qiw), lambda i: (i, col["qi"] // qiw)),
            pl.BlockSpec((QB, LANES), lambda i: (i, col["kw"] // LANES)),
            _resident_spec((seq, LANES)),
            _resident_spec((seq, KV_HEADS * HEAD_DIM)),
            _resident_spec((seq, KV_HEADS * HEAD_DIM)),
            _resident_spec(bias.shape),
        ],
        out_specs=pl.BlockSpec((QB, qw), lambda i: (i, 0)),
        out_shape=jax.ShapeDtypeStruct((seq, qw), BF16),
        scratch_shapes=[
            pltpu.VMEM((LANES, IDX_HEADS * QB), BF16),
            pltpu.VMEM((LANES, QB), F32),
            pltpu.VMEM((seq, QB), I32),
            pltpu.VMEM((KV_HEADS, rows, 2 * HEAD_DIM), BF16),
            pltpu.VMEM((KV_HEADS, rows, LANES), F32),
            pltpu.VMEM((KV_HEADS, rows, 2 * HEAD_DIM), F32),
            pltpu.VMEM((2, KV_HEADS, rows, KC), BF16),
            pltpu.VMEM((2, KV_HEADS, rows, KC), F32),
        ],
        compiler_params=_cparams(("arbitrary",), 56),
        name="dsa_prompt",
    )(z, z, z, kiw_bf, k_bf, v_bf, bias)


def _dsa_sample_body(pt_ref, qi_ref, w_ref, q_ref, bias_ref, *refs, n_pages, ns, t_valid, topk):
    del pt_ref
    np1 = n_pages + 1
    n_in = ns * n_pages + 1
    kidx_refs, k_refs, v_refs = refs[0:n_in], refs[n_in:2 * n_in], refs[2 * n_in:3 * n_in]
    o_ref, key_ref, s_ref = refs[3 * n_in:]
    past = n_pages * PAGE_SIZE
    width = np1 * PAGE_SIZE
    rows = KV_HEADS * Q_PER_KV * SEQ_PAD
    n_th = SEQ_PAD * IDX_HEADS

    for s in range(ns):
        qi = qi_ref[s].astype(BF16)
        w = jnp.broadcast_to(w_ref[s] * IDX_SCALE, (n_th, n_th)).T[:, 0:1]
        for p in range(np1):
            page = kidx_refs[s * n_pages + p][...] if p < n_pages else kidx_refs[n_in - 1][s]
            x = jnp.dot(qi, page.astype(BF16), preferred_element_type=F32)
            sc = (jnp.maximum(x, 0.0) * w).reshape(SEQ_PAD, IDX_HEADS, PAGE_SIZE).sum(axis=1)
            kpos = p * PAGE_SIZE + lax.broadcasted_iota(I32, (SEQ_PAD, PAGE_SIZE), 1)
            qpos = past + lax.broadcasted_iota(I32, (SEQ_PAD, PAGE_SIZE), 0)
            ok = (kpos <= qpos) & (kpos < past + t_valid)
            key_ref[s * SEQ_PAD:(s + 1) * SEQ_PAD, p * PAGE_SIZE:(p + 1) * PAGE_SIZE] = jnp.where(
                ok, _sortable_key(sc), INT_MIN)

    keys = key_ref[...]
    kpos = lax.broadcasted_iota(I32, (ns * SEQ_PAD, width), 1)

    def count_ge(cand):
        return jnp.sum(jnp.where(keys >= cand, 1.0, 0.0), axis=1, keepdims=True)

    def count_tie_lt(t, jc):
        return jnp.sum(jnp.where((keys == t) & (kpos < jc), 1.0, 0.0), axis=1, keepdims=True)

    t, j = _threshold_search(count_ge, count_tie_lt, (ns * SEQ_PAD, 1), topk, (width - 1).bit_length(), width)
    sel = (keys > t) | ((keys == t) & (kpos <= j))
    madd_all = jnp.where(sel, 0.0, NEG)

    def kv_tile(page_refs, s, p, g):
        if p < n_pages:
            return page_refs[s * n_pages + p][pl.ds(g, PAGE_SIZE, stride=KV_HEADS), :].astype(BF16)
        return page_refs[n_in - 1][s, :, g * HEAD_DIM:(g + 1) * HEAD_DIM].astype(BF16)

    grp_rows = Q_PER_KV * SEQ_PAD
    for s in range(ns):
        madd = jnp.concatenate([madd_all[s * SEQ_PAD:(s + 1) * SEQ_PAD]] * (KV_HEADS * Q_PER_KV), axis=0)
        for g in range(KV_HEADS):
            rs = slice(g * grp_rows, (g + 1) * grp_rows)
            qg = (q_ref[s, rs, :] * ATT_SCALE).astype(BF16)
            for p in range(np1):
                s_ref[s, rs, p * PAGE_SIZE:(p + 1) * PAGE_SIZE] = lax.dot_general(
                    qg, kv_tile(k_refs, s, p, g), NT_DIMS, preferred_element_type=F32)
        pr, l = _softmax_rows(s_ref[s] + bias_ref[...] + madd)
        for g in range(KV_HEADS):
            rs = slice(g * grp_rows, (g + 1) * grp_rows)
            acc = jnp.zeros((grp_rows, HEAD_DIM), F32)
            for p in range(np1):
                acc = acc + jnp.dot(pr[rs, p * PAGE_SIZE:(p + 1) * PAGE_SIZE].astype(BF16),
                                    kv_tile(v_refs, s, p, g), preferred_element_type=F32)
            o_ref[s, rs, :] = acc / l[rs]


def _sample_bias(rel_bias, past, width):
    near = width - (past - PAGE_SIZE)
    b = _toeplitz_bias(rel_bias, SEQ_PAD, near, PAGE_SIZE).reshape(N_HEADS_A * SEQ_PAD, near)
    return jnp.pad(b, ((0, 0), (width - near, 0)))


def dsa_sample(qi8, w8, q8, kidx_new, k_new, v_new, cache_kidx, cache_k, cache_v, layer, page_table,
               rel_bias, t_valid, ns=2):
    b, n_pages = page_table.shape
    np1 = n_pages + 1
    past = n_pages * PAGE_SIZE
    width = np1 * PAGE_SIZE
    topk = min(TOPK_MAX, (past + t_valid) // 4)
    bias = _sample_bias(rel_bias, past, width)
    rows = N_HEADS_A * SEQ_PAD
    kvw = KV_HEADS * HEAD_DIM

    assert b % ns == 0

    def page_spec(tail, s, p):
        zeros = (0,) * len(tail)
        return pl.BlockSpec((None, None) + tail, lambda i, pt: (layer, pt[i * ns + s, p]) + zeros)

    def new_spec(tail):
        return pl.BlockSpec((ns,) + tail, lambda i, pt: (i, 0, 0))

    in_specs = [
        pl.BlockSpec((ns, SEQ_PAD * IDX_HEADS, IDX_DIM), lambda i, pt: (i, 0, 0)),
        pl.BlockSpec((ns, 1, SEQ_PAD * IDX_HEADS), lambda i, pt: (i, 0, 0)),
        pl.BlockSpec((ns, rows, HEAD_DIM), lambda i, pt: (i, 0, 0)),
        pl.BlockSpec((rows, width), lambda i, pt: (0, 0)),
    ]
    args = [qi8, w8, q8, bias]
    kv_tail = (PAGE_SIZE * KV_HEADS, HEAD_DIM)
    cache_k = cache_k.reshape(cache_k.shape[:2] + kv_tail)
    cache_v = cache_v.reshape(cache_v.shape[:2] + kv_tail)
    kidx_tail = (IDX_DIM, PAGE_SIZE)
    cache_kidx = jnp.swapaxes(cache_kidx, 2, 3)
    kidx_new = jnp.swapaxes(kidx_new, 1, 2)
    new_tail = (PAGE_SIZE, kvw)
    for arr_cache, arr_new, tail, ntail in ((cache_kidx, kidx_new, kidx_tail, kidx_tail),
                                            (cache_k, k_new, kv_tail, new_tail), (cache_v, v_new, kv_tail, new_tail)):
        in_specs += [page_spec(tail, s, p) for s in range(ns) for p in range(n_pages)] + [new_spec(ntail)]
        args += [arr_cache] * (ns * n_pages) + [arr_new]
    return pl.pallas_call(
        functools.partial(_dsa_sample_body, n_pages=n_pages, ns=ns, t_valid=t_valid, topk=topk),
        grid_spec=pltpu.PrefetchScalarGridSpec(
            num_scalar_prefetch=1,
            grid=(b // ns,),
            in_specs=in_specs,
            out_specs=pl.BlockSpec((ns, rows, HEAD_DIM), lambda i, pt: (i, 0, 0)),
            scratch_shapes=[pltpu.VMEM((ns * SEQ_PAD, width), I32), pltpu.VMEM((ns, rows, width), F32)],
        ),
        out_shape=jax.ShapeDtypeStruct((b, rows, HEAD_DIM), F32),
        compiler_params=_cparams(("arbitrary",), 56),
        name="dsa_sample",
    )(page_table, *args)


def _conv_body(x_ref, st_ref, w_ref, b_ref, o_ref, tail_ref, *, rows):
    @pl.when(pl.program_id(2) == 0)
    def _():
        tail_ref[...] = st_ref[...]

    x = x_ref[...]
    xc = jnp.concatenate([tail_ref[...], x], axis=0)
    out = b_ref[...] + x * w_ref[CONV_W - 1:CONV_W, :]
    for k in range(1, CONV_W):
        shifted = pltpu.roll(xc, k, 0)[SUBLANES:SUBLANES + rows]
        out = out + shifted * w_ref[CONV_W - 1 - k:CONV_W - k, :]
    o_ref[...] = out * _sigmoid(out)
    tail_ref[...] = x[rows - SUBLANES:rows]


def conv_silu(z3, col0, t, state8, conv_w, conv_b, rows, cdim, cb=512):
    b = z3.shape[0]
    assert col0 % cb == 0 and cdim % cb == 0 and t % rows == 0
    return pl.pallas_call(
        functools.partial(_conv_body, rows=rows),
        grid=(b, cdim // cb, t // rows),
        in_specs=[
            pl.BlockSpec((None, rows, cb), lambda i, j, c: (i, c, col0 // cb + j)),
            pl.BlockSpec((None, SUBLANES, cb), lambda i, j, c: (i, 0, j)),
            pl.BlockSpec((CONV_W, cb), lambda i, j, c: (0, j)),
            pl.BlockSpec((1, cb), lambda i, j, c: (0, j)),
        ],
        out_specs=pl.BlockSpec((None, rows, cb), lambda i, j, c: (i, c, j)),
        out_shape=jax.ShapeDtypeStruct((b, t, cdim), F32),
        scratch_shapes=[pltpu.VMEM((SUBLANES, cb), F32)],
        compiler_params=_cparams(("arbitrary", "arbitrary", "arbitrary"), 32),
        name="conv_silu",
    )(z3, state8, conv_w, conv_b.reshape(1, cdim))


def _ssd_pair(xs, dt_raw, dtb, a, dskip, cb, lmask, valid):
    dt = _softplus(dt_raw + dtb)
    if valid is not None:
        dt = jnp.where(valid, dt, 0.0)
    xdt = xs * dt
    acs = jnp.dot(jnp.where(lmask, 1.0, 0.0), dt * a, precision=HI, preferred_element_type=F32)
    acs_t = acs.T
    half = SSM_HEADDIM
    xdt_bf = xdt.astype(BF16)
    yd = []
    for lane0 in (0, half):
        seg = acs[:, lane0:lane0 + 1] - acs_t[lane0:lane0 + 1, :]
        lm = jnp.where(lmask, jnp.exp(jnp.where(lmask, seg, 0.0)), 0.0)
        yd.append(jnp.dot((cb * lm).astype(BF16), xdt_bf, preferred_element_type=F32))
    lane = lax.broadcasted_iota(I32, xs.shape, 1)
    y = jnp.where(lane < half, yd[0], yd[1]) + dskip * xs
    return xdt, acs, y


def _ssd_prompt_body(xs_ref, bm_ref, cm_ref, dt_ref, dtb_ref, a_ref, dsk_ref, y_ref, hf_ref, st_ref,
                     *, n_pairs, n_chunks):
    c = pl.program_id(0)

    @pl.when(c == 0)
    def _():
        st_ref[...] = jnp.zeros_like(st_ref)

    ll = SSD_L
    li = lax.broadcasted_iota(I32, (ll, ll), 0)
    si = lax.broadcasted_iota(I32, (ll, ll), 1)
    lmask = si <= li
    pairs_per_group = n_pairs // SSM_GROUPS
    for g in range(SSM_GROUPS):
        gs = slice(g * D_STATE, (g + 1) * D_STATE)
        bm = bm_ref[:, gs].astype(BF16)
        cm = cm_ref[:, gs].astype(BF16)
        cb = lax.dot_general(cm, bm, NT_DIMS, preferred_element_type=F32)
        for kk in range(pairs_per_group):
            k = g * pairs_per_group + kk
            ks = slice(k * LANES, (k + 1) * LANES)
            xdt, acs, y = _ssd_pair(xs_ref[:, ks], dt_ref[:, ks], dtb_ref[:, ks], a_ref[:, ks],
                                    dsk_ref[:, ks], cb, lmask, None)
            acs_last = acs[ll - 1:ll, :]
            state = st_ref[k]
            y_off = lax.dot_general(cm, state.astype(BF16), NT_DIMS, preferred_element_type=F32)
            y_ref[:, ks] = y + y_off * jnp.exp(acs)
            xd_t = (xdt * jnp.exp(acs_last - acs)).T
            upd = jnp.dot(xd_t.astype(BF16), bm, preferred_element_type=F32)
            cd = jnp.exp(jnp.broadcast_to(acs_last, (ll, LANES))).T[:, 0:1]
            st_ref[k] = state * cd + upd

    @pl.when(c == n_chunks - 1)
    def _():
        hf_ref[...] = st_ref[...]


def ssd_prompt(xc, dt_exp, dtb_exp, a_exp, dsk_exp, d_inner):
    t = xc.shape[0]
    n_pairs = d_inner // LANES
    gn = SSM_GROUPS * D_STATE
    n_chunks = t // SSD_L
    y, hf = pl.pallas_call(
        functools.partial(_ssd_prompt_body, n_pairs=n_pairs, n_chunks=n_chunks),
        grid=(n_chunks,),
        in_specs=[
            pl.BlockSpec((SSD_L, d_inner), lambda c: (c, 0)),
            pl.BlockSpec((SSD_L, gn), lambda c: (c, d_inner // gn)),
            pl.BlockSpec((SSD_L, gn), lambda c: (c, d_inner // gn + 1)),
            pl.BlockSpec((SSD_L, d_inner), lambda c: (c, 0)),
            _const_spec((1, d_inner)),
            _const_spec((1, d_inner)),
            _const_spec((1, d_inner)),
        ],
        out_specs=[
            pl.BlockSpec((SSD_L, d_inner), lambda c: (c, 0)),
            _const_spec((n_pairs, LANES, D_STATE)),
        ],
        out_shape=[
            jax.ShapeDtypeStruct((t, d_inner), F32),
            jax.ShapeDtypeStruct((n_pairs, LANES, D_STATE), F32),
        ],
        scratch_shapes=[pltpu.VMEM((n_pairs, LANES, D_STATE), F32)],
        compiler_params=_cparams(("arbitrary",), 32),
        name="ssd_prompt",
    )(xc, xc, xc, dt_exp, dtb_exp, a_exp, dsk_exp)
    return y, hf.reshape(2 * n_pairs, SSM_HEADDIM, D_STATE)


def _ssd_sample_body(xs_ref, bm_ref, cm_ref, dt_ref, dtb_ref, a_ref, dsk_ref, h0_ref, y_ref, h1_ref,
                     *, n_seq, t_valid):
    ll = n_seq * SEQ_PAD
    li = lax.broadcasted_iota(I32, (ll, ll), 0)
    si = lax.broadcasted_iota(I32, (ll, ll), 1)
    same = lax.shift_right_logical(li, 3) == lax.shift_right_logical(si, 3)
    lmask = same & (si <= li)
    last = same & ((si & (SEQ_PAD - 1)) == SEQ_PAD - 1)
    rowi = lax.broadcasted_iota(I32, (ll, LANES), 0)
    valid = (rowi & (SEQ_PAD - 1)) < t_valid
    bm = bm_ref[...].astype(BF16)
    cm = cm_ref[...].astype(BF16)
    cb = lax.dot_general(cm, bm, NT_DIMS, preferred_element_type=F32)
    xdt, acs, y = _ssd_pair(xs_ref[...], dt_ref[...], dtb_ref[...], a_ref[...], dsk_ref[...],
                            cb, lmask, valid)
    acs_last = jnp.dot(jnp.where(last, 1.0, 0.0), acs, precision=HI, preferred_element_type=F32)
    e_acs = jnp.exp(acs)
    xd_t = (xdt * jnp.exp(acs_last - acs)).T
    cd_t = jnp.exp(acs_last).T
    lane = lax.broadcasted_iota(I32, (LANES, ll), 1)
    cm32 = cm_ref[...]
    y_off = []
    for s in range(n_seq):
        rs = slice(s * SEQ_PAD, (s + 1) * SEQ_PAD)
        state = h0_ref[s].reshape(LANES, D_STATE)
        y_off.append(lax.dot_general(cm32[rs].astype(BF16), state.astype(BF16), NT_DIMS,
                                     preferred_element_type=F32) * e_acs[rs])
        in_seq = (lane >= s * SEQ_PAD) & (lane < (s + 1) * SEQ_PAD)
        upd = jnp.dot(jnp.where(in_seq, xd_t, 0.0).astype(BF16), bm, preferred_element_type=F32)
        new = state * cd_t[:, s * SEQ_PAD:s * SEQ_PAD + 1] + upd
        h1_ref[s] = new.reshape(2, SSM_HEADDIM, D_STATE)
    y_ref[...] = y + jnp.concatenate(y_off, axis=0)


def ssd_sample(xc, dt_exp, dtb_exp, a_exp, dsk_exp, h0, d_inner, t_valid, n_seq=16):
    rows = xc.shape[0]
    b = h0.shape[0]
    n_pairs = d_inner // LANES
    ppg = n_pairs // SSM_GROUPS
    ll = n_seq * SEQ_PAD
    first_b = d_inner // D_STATE
    return pl.pallas_call(
        functools.partial(_ssd_sample_body, n_seq=n_seq, t_valid=t_valid),
        grid=(b // n_seq, n_pairs),
        in_specs=[
            pl.BlockSpec((ll, LANES), lambda s, k: (s, k)),
            pl.BlockSpec((ll, D_STATE), lambda s, k: (s, first_b + k // ppg)),
            pl.BlockSpec((ll, D_STATE), lambda s, k: (s, first_b + SSM_GROUPS + k // ppg)),
            pl.BlockSpec((ll, LANES), lambda s, k: (s, k)),
            pl.BlockSpec((1, LANES), lambda s, k: (0, k)),
            pl.BlockSpec((1, LANES), lambda s, k: (0, k)),
            pl.BlockSpec((1, LANES), lambda s, k: (0, k)),
            pl.BlockSpec((n_seq, 2, SSM_HEADDIM, D_STATE), lambda s, k: (s, k, 0, 0)),
        ],
        out_specs=[
            pl.BlockSpec((ll, LANES), lambda s, k: (s, k)),
            pl.BlockSpec((n_seq, 2, SSM_HEADDIM, D_STATE), lambda s, k: (s, k, 0, 0)),
        ],
        out_shape=[
            jax.ShapeDtypeStruct((rows, d_inner), F32),
            jax.ShapeDtypeStruct(h0.shape, F32),
        ],
        compiler_params=_cparams(("arbitrary", "arbitrary"), 32),
        name="ssd_sample",
    )(xc, xc, xc, dt_exp, dtb_exp, a_exp, dsk_exp, h0)


def _gated_norm_body(y_ref, z_ref, g_ref, o_ref, *, d_inner):
    z = z_ref[...]
    yg = y_ref[...] * (z * _sigmoid(z))
    gw = d_inner // SSM_GROUPS
    for g in range(SSM_GROUPS):
        gs = slice(g * gw, (g + 1) * gw)
        v = yg[:, gs]
        r = lax.rsqrt(jnp.mean(v * v, axis=-1, keepdims=True) + EPS)
        o_ref[:, gs] = (v * r * g_ref[:, gs]).astype(BF16)


def gated_norm(y, z, norm_g, d_inner, tm=TM):
    t = y.shape[0]
    return pl.pallas_call(
        functools.partial(_gated_norm_body, d_inner=d_inner),
        grid=(t // tm,),
        in_specs=[
            pl.BlockSpec((tm, d_inner), lambda i: (i, 0)),
            pl.BlockSpec((tm, d_inner), lambda i: (i, 0)),
            _const_spec((1, d_inner)),
        ],
        out_specs=pl.BlockSpec((tm, d_inner), lambda i: (i, 0)),
        out_shape=jax.ShapeDtypeStruct((t, d_inner), BF16),
        compiler_params=_cparams(("arbitrary",), 32),
        name="gated_norm",
    )(y, z, norm_g.reshape(1, d_inner))


def _pack_cols(w, pieces, total):
    cols = [w[:, a:b] for a, b in pieces]
    used = sum(b - a for a, b in pieces)
    if total > used:
        cols.append(jnp.zeros((w.shape[0], total - used), w.dtype))
    return jnp.concatenate(cols, axis=1).astype(BF16)


def _pad_rows(a, rows, axis):
    pad = [(0, 0)] * a.ndim
    pad[axis] = (0, rows - a.shape[axis])
    return jnp.pad(a, pad)


def _dsa_layer(x_all, n_prompt, n_seq, t_s, mix_g, w_in, rel_bias, cache_k, cache_v, cache_kidx, layer,
               page_table):
    qw, kvw, qiw = N_HEADS_A * HEAD_DIM, KV_HEADS * HEAD_DIM, IDX_HEADS * IDX_DIM
    o_q, o_k, o_v, o_qi = 0, qw, qw + kvw, qw + 2 * kvw
    o_ki = o_qi + qiw
    o_wi = o_ki + IDX_DIM
    o_qm = o_wi + IDX_HEADS
    col = {"q": 0, "k": qw, "qi": qw + kvw, "v": qw + kvw + qiw, "qm": qw + 2 * kvw + qiw}
    col["kw"] = col["qm"] + MEM_WIDTH
    width = col["kw"] + LANES
    w = _pack_cols(w_in, [(o_q, o_q + qw), (o_k, o_k + kvw), (o_qi, o_qi + qiw), (o_v, o_v + kvw),
                          (o_qm, o_qm + MEM_WIDTH), (o_ki, o_ki + IDX_DIM), (o_wi, o_wi + IDX_HEADS)], width)
    z = norm_matmul(x_all, mix_g, w, TM, width // 3)

    k_all = z[:, col["k"]:col["k"] + kvw]
    v_all = z[:, col["v"]:col["v"] + kvw]
    ki_all = z[:, col["kw"]:col["kw"] + IDX_DIM]

    kiw_bf = z[:n_prompt, col["kw"]:col["kw"] + LANES].astype(BF16)
    mix_p = dsa_prompt(z, n_prompt, col, kiw_bf, k_all[:n_prompt].astype(BF16),
                       v_all[:n_prompt].astype(BF16), rel_bias)

    zs = z[n_prompt:].reshape(n_seq, t_s, width)
    zs8 = _pad_rows(zs, SEQ_PAD, 1)
    qi8 = zs8[:, :, col["qi"]:col["qi"] + qiw].reshape(n_seq, SEQ_PAD * IDX_HEADS, IDX_DIM)
    w8 = zs8[:, :, col["kw"] + IDX_DIM:col["kw"] + IDX_DIM + IDX_HEADS].reshape(n_seq, 1, SEQ_PAD * IDX_HEADS)
    q8 = zs8[:, :, :qw].reshape(n_seq, SEQ_PAD, N_HEADS_A, HEAD_DIM).transpose(0, 2, 1, 3)
    q8 = q8.reshape(n_seq, N_HEADS_A * SEQ_PAD, HEAD_DIM)
    kidx_new = _pad_rows(zs[:, :, col["kw"]:col["kw"] + IDX_DIM], PAGE_SIZE, 1)
    k_new = _pad_rows(zs[:, :, col["k"]:col["k"] + kvw], PAGE_SIZE, 1)
    v_new = _pad_rows(zs[:, :, col["v"]:col["v"] + kvw], PAGE_SIZE, 1)
    o_s = dsa_sample(qi8, w8, q8, kidx_new, k_new, v_new, cache_kidx, cache_k, cache_v, layer,
                     page_table, rel_bias, t_s)
    mix_s = o_s.reshape(n_seq, N_HEADS_A, SEQ_PAD, HEAD_DIM)[:, :, :t_s].transpose(0, 2, 1, 3)
    mix_s = mix_s.reshape(n_seq * t_s, qw).astype(BF16)
    return z, col["qm"] // MEM_WIDTH, jnp.concatenate([mix_p, mix_s], axis=0), k_all, v_all, ki_all


def _ssd_layer(x_all, n_prompt, n_seq, t_s, mix_g, w_in, conv_w, conv_b, dt_bias, a_log, d_skip, norm_g,
               state_conv, state_ssm):
    n_heads = dt_bias.shape[0]
    d_inner = n_heads * SSM_HEADDIM
    cdim = d_inner + 2 * SSM_GROUPS * D_STATE
    o_x, o_dt = d_inner, d_inner + cdim
    o_qm = o_dt + n_heads
    c_qm = d_inner + cdim
    c_dt = c_qm + MEM_WIDTH
    width = _round_up(c_dt + n_heads, 2 * LANES)
    w = _pack_cols(w_in, [(0, d_inner), (o_x, o_x + cdim), (o_qm, o_qm + MEM_WIDTH), (o_dt, o_dt + n_heads)], width)
    z = norm_matmul(x_all, mix_g, w, TM, width // 2)
    xbc = z[:, d_inner:d_inner + cdim]
    dt_exp = jnp.repeat(z[:, c_dt:c_dt + n_heads], SSM_HEADDIM, axis=1)
    dtb_exp = jnp.repeat(dt_bias.astype(F32), SSM_HEADDIM).reshape(1, d_inner)
    a_exp = -jnp.exp(jnp.repeat(a_log.astype(F32), SSM_HEADDIM)).reshape(1, d_inner)
    dsk_exp = jnp.repeat(d_skip.astype(F32), SSM_HEADDIM).reshape(1, d_inner)

    xc_p = conv_silu(z[None], d_inner, n_prompt, jnp.zeros((1, SUBLANES, cdim), F32), conv_w, conv_b,
                     8 * SSD_L, cdim)[0]
    y_p, hf_p = ssd_prompt(xc_p, dt_exp, dtb_exp, a_exp, dsk_exp, d_inner)
    new_conv_p = xbc[n_prompt - (CONV_W - 1):n_prompt]

    xbc_s = xbc[n_prompt:].reshape(n_seq, t_s, cdim)
    xbc_s8 = _pad_rows(xbc_s, SEQ_PAD, 1)
    st8 = jnp.concatenate([jnp.zeros((n_seq, SUBLANES - (CONV_W - 1), cdim), F32), state_conv.astype(F32)], axis=1)
    xc_s = conv_silu(xbc_s8, 0, SEQ_PAD, st8, conv_w, conv_b, SEQ_PAD, cdim, cb=cdim)
    xc_s = xc_s.reshape(n_seq * SEQ_PAD, cdim)
    dt_s8 = _pad_rows(dt_exp[n_prompt:].reshape(n_seq, t_s, d_inner), SEQ_PAD, 1).reshape(n_seq * SEQ_PAD, d_inner)
    y_s8, hf_s = ssd_sample(xc_s, dt_s8, dtb_exp, a_exp, dsk_exp, state_ssm.astype(F32), d_inner, t_s)
    y_s = y_s8.reshape(n_seq, SEQ_PAD, d_inner)[:, :t_s].reshape(n_seq * t_s, d_inner)
    new_conv_s = jnp.concatenate([state_conv.astype(F32), xbc_s], axis=1)[:, -(CONV_W - 1):]

    mix = gated_norm(jnp.concatenate([y_p, y_s], axis=0), z, norm_g, d_inner)
    return z, c_qm // MEM_WIDTH, mix, hf_p, new_conv_p, hf_s, new_conv_s


def kernel(x_prompt, x_sample, mem_prompt, cache_k, cache_v, cache_kidx, page_table, state_ssm, state_conv,
           cache_mem_k, cache_mem_v, rel_bias, ffn1_g, ffn1_w_gu, ffn1_w_down, mix_g, mem_g, w_mem_kv,
           w_in_attn, w_in_ssd, conv_w, conv_b, dt_bias, a_log, d_skip, ssd_norm_g, w_out,
           ffn2_g, ffn2_w_gu, ffn2_w_down, final_g):
    bp, n_prompt, d = x_prompt.shape
    n_seq, t_s, _ = x_sample.shape
    assert bp == 1
    depth = ffn1_g.shape[0]
    n_mem = mem_prompt.shape[1]
    x_all = jnp.concatenate([x_prompt[0], x_sample.reshape(n_seq * t_s, d)], axis=0)
    outs = {k: [] for k in ("pk", "pv", "pki", "pssm", "pconv", "pmk", "pmv", "sk", "sv", "ski", "sssm", "sconv")}
    y_all = None
    ffn1_w = prep_ffn_weights(ffn1_w_gu, ffn1_w_down, TF)
    ffn2_w = prep_ffn_weights(ffn2_w_gu, ffn2_w_down, TF)
    for i in range(depth):
        j = i // 2
        x_all = ffn(x_all, ffn1_g[i], ffn1_w, i)
        mkv = norm_matmul(mem_prompt[0], mem_g[i], w_mem_kv[i].astype(BF16), n_mem, 2 * MEM_WIDTH)
        outs["pmk"].append(mkv[:, :MEM_WIDTH].reshape(1, n_mem, MEM_HEADS, MEM_HEAD_DIM))
        outs["pmv"].append(mkv[:, MEM_WIDTH:].reshape(1, n_mem, MEM_HEADS, MEM_HEAD_DIM))
        if i % 2 == 0:
            z, qm_blk, mix, k_all, v_all, ki_all = _dsa_layer(
                x_all, n_prompt, n_seq, t_s, mix_g[i], w_in_attn[j], rel_bias,
                cache_k, cache_v, cache_kidx, j, page_table)
            outs["pk"].append(k_all[:n_prompt].reshape(1, n_prompt, KV_HEADS, HEAD_DIM))
            outs["pv"].append(v_all[:n_prompt].reshape(1, n_prompt, KV_HEADS, HEAD_DIM))
            outs["pki"].append(ki_all[:n_prompt].reshape(1, n_prompt, IDX_DIM))
            outs["sk"].append(k_all[n_prompt:].reshape(n_seq, t_s, KV_HEADS, HEAD_DIM))
            outs["sv"].append(v_all[n_prompt:].reshape(n_seq, t_s, KV_HEADS, HEAD_DIM))
            outs["ski"].append(ki_all[n_prompt:].reshape(n_seq, t_s, IDX_DIM))
        else:
            z, qm_blk, mix, hf_p, conv_p, hf_s, conv_s = _ssd_layer(
                x_all, n_prompt, n_seq, t_s, mix_g[i], w_in_ssd[j], conv_w[j], conv_b[j], dt_bias[j],
                a_log[j], d_skip[j], ssd_norm_g[j], state_conv[j], state_ssm[j])
            outs["pssm"].append(hf_p[None])
            outs["pconv"].append(conv_p[None])
            outs["sssm"].append(hf_s)
            outs["sconv"].append(conv_s)
        mem_p = mem_attn_prompt(z, qm_blk, n_prompt, mkv)
        qm_s = z[n_prompt:, qm_blk * MEM_WIDTH:(qm_blk + 1) * MEM_WIDTH].reshape(n_seq, t_s, MEM_WIDTH)
        mem_s = mem_attn_sample(_pad_rows(qm_s, SEQ_PAD, 1), cache_mem_k, cache_mem_v, i)
        mem = jnp.concatenate([mem_p, mem_s[:, :t_s].reshape(n_seq * t_s, MEM_WIDTH)], axis=0)
        x_all = out_proj(x_all, mix, mem, w_out[i])
        if i == depth - 1:
            x_all, y_all = ffn(x_all, ffn2_g[i], ffn2_w, i, final_g=final_g)
        else:
            x_all = ffn(x_all, ffn2_g[i], ffn2_w, i)
    y_prompt = y_all[:n_prompt][None]
    y_sample = y_all[n_prompt:].reshape(n_seq, t_s, d)
    st = lambda k: jnp.stack(outs[k])
    return (y_prompt, y_sample, st("pk"), st("pv"), st("pki"), st("pssm"), st("pconv"), st("pmk"), st("pmv"),
            st("sk"), st("sv"), st("ski"), st("sssm"), st("sconv"))
```
